```python
import jax, jax.numpy as jnp
from jax import lax
import numpy as np

D_MODEL = 2048
BATCH = 8
SEQ = 2048
DEPTH = 4

BLOCK_Q = 128
HEAD_DIM = 128
ROPE_THETA = 10000.0
EPS = 1e-6
N_BRANCH = 4
BRANCH_WIDTH = 512
SB_HEADS = 4
SB_W = SB_HEADS * HEAD_DIM
MLA_HEADS = 4
MLA_NOPE = 128
MLA_ROPE = 64
MLA_V = 128
MLA_KV_RANK = 512
MLA_Q_W = MLA_HEADS * (MLA_NOPE + MLA_ROPE)
RET_HEADS = 4
RET_DK = 128
RET_DV = 128
RET_CHUNK = 128
RET_W = RET_HEADS * RET_DK
RET_VW = RET_HEADS * RET_DV
LRU_WIDTH = 512
LRU_BLOCKS = 4
LRU_BLOCK = LRU_WIDTH // LRU_BLOCKS
CONV_WIDTH = 4
LRU_C = 8.0
D_FF = -(-8 * D_MODEL // (3 * 256)) * 256
SPLIT_SIZES = (SB_W, SB_W, SB_W,
               MLA_Q_W, MLA_KV_RANK, MLA_ROPE,
               RET_W, RET_W, RET_VW, RET_VW,
               LRU_WIDTH, LRU_WIDTH)
IN_WIDTH = sum(SPLIT_SIZES)

kernel_name = "hybrid_gated_sb_mla_ret_rglru_block"


def rmsnorm(x, g):
    xf = x.astype(jnp.float32)
    y = xf * lax.rsqrt(jnp.mean(xf * xf, axis=-1, keepdims=True) + EPS)
    return (y * g.astype(jnp.float32)).astype(x.dtype)


def rope(x, pos):
    d = x.shape[-1]
    freq = ROPE_THETA ** (-jnp.arange(0, d, 2, dtype=jnp.float32) / d)
    ang = pos.astype(jnp.float32)[:, :, None] * freq
    cos = jnp.cos(ang)[:, :, None, :]
    sin = jnp.sin(ang)[:, :, None, :]
    xf = x.astype(jnp.float32)
    x1, x2 = xf[..., : d // 2], xf[..., d // 2:]
    return jnp.concatenate([x1 * cos - x2 * sin, x2 * cos + x1 * sin], axis=-1).astype(x.dtype)


def stick_breaking_attention(q, k, v):
    S, D = q.shape[1], q.shape[-1]
    scale = D ** -0.5
    outs = []
    for i0 in range(0, S, BLOCK_Q):
        L = i0 + BLOCK_Q
        z = jnp.einsum('bqhd,bkhd->bhqk', q[:, i0:L], k[:, :L]).astype(jnp.float32) * scale
        t_idx = i0 + jnp.arange(BLOCK_Q)[:, None]
        s_idx = jnp.arange(L)[None, :]
        mask = s_idx < t_idx
        log_1m_beta = jnp.where(mask, jax.nn.log_sigmoid(-z), 0.0)
        tail = lax.cumsum(log_1m_beta, axis=3, reverse=True) - log_1m_beta
        w = jnp.where(mask, jnp.exp(jax.nn.log_sigmoid(z) + tail), 0.0)
        outs.append(jnp.einsum('bhqk,bkhd->bqhd', w.astype(v.dtype), v[:, :L]))
    return jnp.concatenate(outs, axis=1)


def latent_attention(q, c_kv, k_rope, pos, kv_gain, w_ukv):
    B, S = q.shape[0], q.shape[1]
    q_nope = q[..., :MLA_NOPE]
    q_pe = rope(q[..., MLA_NOPE:], pos)
    k_pe = rope(k_rope[:, :, None, :], pos)[:, :, 0, :]
    kv = (rmsnorm(c_kv, kv_gain) @ w_ukv).reshape(B, S, MLA_HEADS, MLA_NOPE + MLA_V)
    k_nope, v = kv[..., :MLA_NOPE], kv[..., MLA_NOPE:]
    scale = (MLA_NOPE + MLA_ROPE) ** -0.5
    outs = []
    for i0 in range(0, S, BLOCK_Q):
        L = i0 + BLOCK_Q
        s = (jnp.einsum('bqhd,bkhd->bhqk', q_nope[:, i0:L], k_nope[:, :L])
             + jnp.einsum('bqhd,bkd->bhqk', q_pe[:, i0:L], k_pe[:, :L])).astype(jnp.float32) * scale
        mask = jnp.arange(L)[None, :] <= (i0 + jnp.arange(BLOCK_Q))[:, None]
        p = jax.nn.softmax(jnp.where(mask, s, -jnp.inf), axis=-1)
        outs.append(jnp.einsum('bhqk,bkhd->bqhd', p.astype(v.dtype), v[:, :L]))
    return jnp.concatenate(outs, axis=1).reshape(B, S, MLA_HEADS * MLA_V)


def retention(q, k, v, g, pos):
    B, S, H, DK = q.shape
    DV = v.shape[-1]
    C = RET_CHUNK
    N = S // C
    q = rope(q, pos).astype(jnp.float32)
    k = rope(k, pos).astype(jnp.float32) * DK ** -0.5
    log_gamma = jnp.log1p(-(2.0 ** (-5.0 - jnp.arange(H, dtype=jnp.float32))))
    qc = q.reshape(B, N, C, H, DK)
    kc = k.reshape(B, N, C, H, DK)
    vc = v.astype(jnp.float32).reshape(B, N, C, H, DV)
    idx = jnp.arange(C, dtype=jnp.float32)
    diff = idx[:, None] - idx[None, :]
    decay = jnp.where(diff >= 0, jnp.exp(log_gamma[:, None, None] * jnp.maximum(diff, 0.0)), 0.0)
    scores = jnp.einsum('bnqhd,bnkhd->bnhqk', qc, kc) * decay
    inner = jnp.einsum('bnhqk,bnkhe->bnqhe', scores, vc)
    k_decay = jnp.exp(log_gamma[:, None] * (C - 1.0 - idx)[None, :])
    chunk_kv = jnp.einsum('bnkhd,hk,bnkhe->bnhde', kc, k_decay, vc)
    chunk_decay = jnp.exp(log_gamma * C)[None, :, None, None]

    def step(state, kv_i):
        return state * chunk_decay + kv_i, state

    init = jnp.zeros((B, H, DK, DV), jnp.float32)
    _, prev = lax.scan(step, init, jnp.moveaxis(chunk_kv, 1, 0))
    prev = jnp.moveaxis(prev, 0, 1)
    q_decay = jnp.exp(log_gamma[:, None] * (idx + 1.0)[None, :])
    cross = jnp.einsum('bnqhd,hq,bnhde->bnqhe', qc, q_decay, prev)
    o = (inner + cross).reshape(B, S, H, DV)
    o = o * lax.rsqrt(jnp.mean(o * o, axis=-1, keepdims=True) + EPS)
    o = o.reshape(B, S, H * DV) * jax.nn.silu(g.astype(jnp.float32))
    return o.astype(v.dtype)


def rglru_block(x_in, y_in, conv_w, conv_b, w_a, b_a, w_x, b_x, lam):
    B, S, W = x_in.shape
    xf = x_in.astype(jnp.float32)
    xp = jnp.pad(xf, ((0, 0), (CONV_WIDTH - 1, 0), (0, 0)))
    xc = conv_b.astype(jnp.float32) + xp[:, 0:S] * conv_w[0]
    for tap in range(1, CONV_WIDTH):
        xc = xc + xp[:, tap:tap + S] * conv_w[tap]
    xb = xc.reshape(B, S, LRU_BLOCKS, LRU_BLOCK)
    r = jax.nn.sigmoid(jnp.einsum('bshi,hij->bshj', xb, w_a.astype(jnp.float32)).reshape(B, S, W) + b_a)
    i = jax.nn.sigmoid(jnp.einsum('bshi,hij->bshj', xb, w_x.astype(jnp.float32)).reshape(B, S, W) + b_x)
    log_a = -LRU_C * r * jax.nn.softplus(-lam.astype(jnp.float32))
    a = jnp.exp(log_a)
    b = jnp.sqrt(-jnp.expm1(2.0 * log_a)) * (i * xc)

    def combine(left, right):
        a1, b1 = left
        a2, b2 = right
        return a1 * a2, a2 * b1 + b2

    _, h = lax.associative_scan(combine, (a, b), axis=1)
    return (h * jax.nn.gelu(y_in.astype(jnp.float32))).astype(x_in.dtype)


def hybrid_layer(x, positions, ln1, w_in, kv_gain, w_ukv, conv_w, conv_b, w_a, b_a, w_x, b_x, lam,
                 w_branch, w_gate, w_out, ln2, wf_gate, wf_up, wf_down):
    B, S, _ = x.shape
    h = rmsnorm(x, ln1)
    proj = h @ w_in
    bounds = np.cumsum(SPLIT_SIZES)[:-1].tolist()
    (sb_q, sb_k, sb_v, mla_q, mla_ckv, mla_kr,
     ret_q, ret_k, ret_v, ret_g, lru_x, lru_y) = jnp.split(proj, bounds, axis=-1)

    def heads(t, n):
        return t.reshape(B, S, n, -1)

    o_sb = stick_breaking_attention(heads(sb_q, SB_HEADS), heads(sb_k, SB_HEADS),
                                    heads(sb_v, SB_HEADS)).reshape(B, S, SB_W)
    o_mla = latent_attention(heads(mla_q, MLA_HEADS), mla_ckv, mla_kr, positions, kv_gain, w_ukv)
    o_ret = retention(heads(ret_q, RET_HEADS), heads(ret_k, RET_HEADS), heads(ret_v, RET_HEADS),
                      ret_g, positions)
    o_lru = rglru_block(lru_x, lru_y, conv_w, conv_b, w_a, b_a, w_x, b_x, lam)
    branches = (o_sb, o_mla, o_ret, o_lru)

    mixed = jax.nn.sigmoid((h @ w_gate[0]).astype(jnp.float32)) * (branches[0] @ w_branch[0])
    for br in range(1, N_BRANCH):
        gate = jax.nn.sigmoid((h @ w_gate[br]).astype(jnp.float32))
        mixed = mixed + gate * (branches[br] @ w_branch[br])
    x = x + mixed.astype(x.dtype) @ w_out

    h2 = rmsnorm(x, ln2)
    x = x + (jax.nn.silu(h2 @ wf_gate) * (h2 @ wf_up)) @ wf_down
    return x


def setup_inputs(seed: int = 0) -> dict:
    key = jax.random.key(seed)
    ks = jax.random.split(key, 24)
    f32 = jnp.float32
    D = D_MODEL

    def nrm(k, shape, fan_in):
        return jax.random.normal(k, shape, f32) * fan_in ** -0.5

    def gain(k, shape):
        return 1.0 + 0.05 * jax.random.normal(k, shape, f32)

    u = jax.random.uniform(ks[10], (DEPTH, LRU_WIDTH), f32, minval=0.9, maxval=0.999)
    s = u ** (1.0 / LRU_C)
    lam = jnp.log(s) - jnp.log1p(-s)
    return {
        "x": jax.random.normal(ks[0], (BATCH, SEQ, D), f32),
        "positions": jnp.tile(jnp.arange(SEQ, dtype=jnp.int32)[None, :], (BATCH, 1)),
        "ln1": gain(ks[1], (DEPTH, D)),
        "w_in": nrm(ks[2], (DEPTH, D, IN_WIDTH), D),
        "mla_kv_gain": gain(ks[3], (DEPTH, MLA_KV_RANK)),
        "mla_w_ukv": nrm(ks[4], (DEPTH, MLA_KV_RANK, MLA_HEADS * (MLA_NOPE + MLA_V)), MLA_KV_RANK),
        "lru_conv_w": nrm(ks[5], (DEPTH, CONV_WIDTH, LRU_WIDTH), CONV_WIDTH),
        "lru_conv_b": 0.01 * jax.random.normal(ks[6], (DEPTH, LRU_WIDTH), f32),
        "lru_w_a": nrm(ks[7], (DEPTH, LRU_BLOCKS, LRU_BLOCK, LRU_BLOCK), LRU_BLOCK),
        "lru_b_a": 0.01 * jax.random.normal(ks[8], (DEPTH, LRU_WIDTH), f32),
        "lru_w_x": nrm(ks[9], (DEPTH, LRU_BLOCKS, LRU_BLOCK, LRU_BLOCK), LRU_BLOCK),
        "lru_b_x": 0.01 * jax.random.normal(ks[11], (DEPTH, LRU_WIDTH), f32),
        "lru_lambda": lam,
        "w_branch": nrm(ks[12], (DEPTH, N_BRANCH, BRANCH_WIDTH, D), BRANCH_WIDTH),
        "w_gate": nrm(ks[13], (DEPTH, N_BRANCH, D, D), D),
        "w_out": nrm(ks[14], (DEPTH, D, D), D),
        "ln2": gain(ks[15], (DEPTH, D)),
        "ffn_w_gate": nrm(ks[16], (DEPTH, D, D_FF), D),
        "ffn_w_up": nrm(ks[17], (DEPTH, D, D_FF), D),
        "ffn_w_down": nrm(ks[18], (DEPTH, D_FF, D), D_FF),
        "ln_final": gain(ks[19], (D,)),
    }


def reference(x, positions, ln1, w_in, mla_kv_gain, mla_w_ukv, lru_conv_w, lru_conv_b, lru_w_a,
              lru_b_a, lru_w_x, lru_b_x, lru_lambda, w_branch, w_gate, w_out, ln2,
              ffn_w_gate, ffn_w_up, ffn_w_down, ln_final):
    for layer in range(DEPTH):
        x = hybrid_layer(x, positions, ln1[layer], w_in[layer], mla_kv_gain[layer], mla_w_ukv[layer],
                         lru_conv_w[layer], lru_conv_b[layer], lru_w_a[layer], lru_b_a[layer],
                         lru_w_x[layer], lru_b_x[layer], lru_lambda[layer], w_branch[layer],
                         w_gate[layer], w_out[layer], ln2[layer], ffn_w_gate[layer],
                         ffn_w_up[layer], ffn_w_down[layer])
    return rmsnorm(x, ln_final)
```

```python
import functools

import numpy as np
import jax
import jax.numpy as jnp
from jax import lax
from jax.experimental import pallas as pl
from jax.experimental.pallas import tpu as pltpu

F32 = jnp.float32
BF16 = jnp.bfloat16

D_MODEL = 2048
BATCH = 8
SEQ = 2048
DEPTH = 4
ROWS = BATCH * SEQ
HEAD_DIM = 128
HEADS = 4
ROPE_THETA = 10000.0
EPS = 1e-6
N_BRANCH = 4
BRANCH_WIDTH = 512
MLA_NOPE = 128
MLA_ROPE = 64
MLA_KV_RANK = 512
RET_CHUNK = 128
LRU_WIDTH = 512
LRU_BLOCKS = 4
LRU_BLOCK = 128
CONV_WIDTH = 4
LRU_C = 8.0
D_FF = 5632

LANES = 128
SUBLANES = 8
GROUP = 512
G_SBQ, G_SBK, G_SBV, G_MQN, G_CKV, G_PE, G_RQ, G_RK, G_RV, G_RG, G_LX, G_LY = range(12)
PROJ_W = 12 * GROUP
HB = GROUP // LANES

VMEM_MB = 1024 * 1024


def _cparams(sem, vmem_mb):
    return pltpu.CompilerParams(dimension_semantics=sem, vmem_limit_bytes=vmem_mb * VMEM_MB)


def _dot(a, b):
    return jnp.dot(a, b, preferred_element_type=F32)


def _dot_nt(a, b):
    return lax.dot_general(a, b, (((1,), (1,)), ((), ())), preferred_element_type=F32)


def _dot_tn(a, b):
    return lax.dot_general(a, b, (((0,), (0,)), ((), ())), preferred_element_type=F32)


def _rmsnorm_rows(x, g):
    return x * lax.rsqrt(jnp.mean(x * x, axis=-1, keepdims=True) + EPS) * g


def _tables_kernel(pos_ref, fr_ref, sr_ref, fm_ref, sm_ref, rc_ref, rs_ref, mc_ref, ms_ref):
    pos = pos_ref[...]
    ang = pos * fr_ref[...]
    rc_ref[...] = jnp.cos(ang)
    rs_ref[...] = jnp.sin(ang) * sr_ref[...]
    ang = pos * fm_ref[...]
    mc_ref[...] = jnp.cos(ang)
    ms_ref[...] = jnp.sin(ang) * sm_ref[...]


def _rope_tables(positions):
    tm = 1024
    pos = jnp.broadcast_to(positions.reshape(ROWS, 1).astype(F32), (ROWS, LANES))

    def freq(d):
        return ROPE_THETA ** (-jnp.arange(0, d, 2, dtype=F32) / d)

    def sign(d):
        return jnp.concatenate([-jnp.ones((d // 2,), F32), jnp.ones((d // 2,), F32)])

    fr = jnp.tile(freq(HEAD_DIM), 2).reshape(1, LANES)
    sr = sign(HEAD_DIM).reshape(1, LANES)
    fm = jnp.tile(freq(MLA_ROPE), 4).reshape(1, LANES)
    sm = jnp.tile(sign(MLA_ROPE), 2).reshape(1, LANES)
    row = pl.BlockSpec((tm, LANES), lambda i: (i, 0))
    vec = pl.BlockSpec((1, LANES), lambda i: (0, 0))
    shp = jax.ShapeDtypeStruct((ROWS, LANES), F32)
    return pl.pallas_call(
        _tables_kernel,
        grid=(ROWS // tm,),
        in_specs=[row, vec, vec, vec, vec],
        out_specs=[row, row, row, row],
        out_shape=[shp, shp, shp, shp],
        compiler_params=_cparams(("parallel",), 32),
        name="rope_tables",
    )(pos, fr, sr, fm, sm)


PROJ_TM, PROJ_TN = 1024, 512
NORM_CHUNK = 256


def _norm_into(x_ref, g_ref, h_ref, rows):
    g = g_ref[...]

    def body(c, _):
        r0 = pl.multiple_of(c * NORM_CHUNK, NORM_CHUNK)
        x = x_ref[pl.ds(r0, NORM_CHUNK), :]
        h_ref[pl.ds(r0, NORM_CHUNK), :] = _rmsnorm_rows(x, g).astype(h_ref.dtype)
        return 0

    lax.fori_loop(0, rows // NORM_CHUNK, body, 0)


def _proj_kernel(x_ref, g_ref, w_ref, o_ref, h_ref):
    @pl.when(pl.program_id(1) == 0)
    def _():
        _norm_into(x_ref, g_ref, h_ref, PROJ_TM)

    o_ref[...] = _dot(h_ref[...], w_ref[...])


def _proj(x, ln, w):
    tm, tn = PROJ_TM, PROJ_TN
    return pl.pallas_call(
        _proj_kernel,
        grid=(ROWS // tm, PROJ_W // tn),
        in_specs=[
            pl.BlockSpec((tm, D_MODEL), lambda i, j: (i, 0)),
            pl.BlockSpec((1, D_MODEL), lambda i, j: (0, 0)),
            pl.BlockSpec((D_MODEL, tn), lambda i, j: (0, j)),
        ],
        out_specs=[
            pl.BlockSpec((tm, tn), lambda i, j: (i, j)),
            pl.BlockSpec((tm, D_MODEL), lambda i, j: (i, 0)),
        ],
        out_shape=[
            jax.ShapeDtypeStruct((ROWS, PROJ_W), F32),
            jax.ShapeDtypeStruct((ROWS, D_MODEL), BF16),
        ],
        compiler_params=_cparams(("parallel", "arbitrary"), 48),
        name="norm_proj",
    )(x, ln, w)


SB_TQ = 256
SB_TK = 128


def _sb_kernel(q_ref, k_ref, v_ref, tri_ref, o_ref, kb_ref, vb_ref):
    i = pl.program_id(2)

    @pl.when(i == 0)
    def _():
        kb_ref[...] = k_ref[...].astype(BF16)
        vb_ref[...] = v_ref[...].astype(BF16)

    q = q_ref[...].astype(BF16)
    tri = tri_ref[...]
    scale = HEAD_DIM ** -0.5
    nsub = SB_TQ // SB_TK
    row = lax.broadcasted_iota(jnp.int32, (SB_TQ, SB_TK), 0)
    col = lax.broadcasted_iota(jnp.int32, (SB_TQ, SB_TK), 1)

    def block(kidx, carry, acc, mask):
        k0 = pl.multiple_of(kidx * SB_TK, SB_TK)
        kblk = kb_ref[pl.ds(k0, SB_TK), :]
        vblk = vb_ref[pl.ds(k0, SB_TK), :]
        z = _dot_nt(q, kblk) * scale
        sp = jnp.log1p(jnp.exp(-jnp.abs(z)))
        ls = jnp.minimum(z, 0.0) - sp
        l1 = ls - z
        if mask is not None:
            l1 = jnp.where(mask, l1, 0.0)
        hi = l1.astype(BF16)
        lo = (l1 - hi.astype(F32)).astype(BF16)
        r = _dot(jnp.concatenate([hi, lo], axis=1), tri)
        w = jnp.exp(ls + r[:, :SB_TK] + carry)
        if mask is not None:
            w = jnp.where(mask, w, 0.0)
        acc = acc + _dot(w.astype(BF16), vblk)
        carry = carry + r[:, SB_TK:]
        return carry, acc

    carry = jnp.zeros((SB_TQ, SB_TK), F32)
    acc = jnp.zeros((SB_TQ, HEAD_DIM), F32)
    for s in range(nsub - 1, -1, -1):
        mask = (col + s * SB_TK) < row
        carry, acc = block(i * nsub + s, carry, acc, mask)

    def body(t, c):
        return block(i * nsub - 1 - t, c[0], c[1], None)

    carry, acc = lax.fori_loop(0, i * nsub, body, (carry, acc))
    o_ref[...] = acc.astype(o_ref.dtype)


def _sb_tri():
    j = np.arange(SB_TK)
    t = (j[:, None] > j[None, :]).astype(np.float32)
    half = np.concatenate([t, np.ones((SB_TK, SB_TK), np.float32)], axis=1)
    return jnp.asarray(np.concatenate([half, half], axis=0), dtype=BF16)


def _sb_attention(proj):
    nq = SEQ // SB_TQ
    return pl.pallas_call(
        _sb_kernel,
        grid=(BATCH, HEADS, nq),
        in_specs=[
            pl.BlockSpec((SB_TQ, HEAD_DIM), lambda b, h, i: (b * nq + i, G_SBQ * HB + h)),
            pl.BlockSpec((SEQ, HEAD_DIM), lambda b, h, i: (b, G_SBK * HB + h)),
            pl.BlockSpec((SEQ, HEAD_DIM), lambda b, h, i: (b, G_SBV * HB + h)),
            pl.BlockSpec((2 * SB_TK, 2 * SB_TK), lambda b, h, i: (0, 0)),
        ],
        out_specs=pl.BlockSpec((SB_TQ, HEAD_DIM), lambda b, h, i: (b * nq + i, h)),
        out_shape=jax.ShapeDtypeStruct((ROWS, BRANCH_WIDTH), BF16),
        scratch_shapes=[pltpu.VMEM((SEQ, HEAD_DIM), BF16), pltpu.VMEM((SEQ, HEAD_DIM), BF16)],
        compiler_params=_cparams(("parallel", "parallel", "arbitrary"), 32),
        name="sb_attention",
    )(proj, proj, proj, _sb_tri())


MLA_TM = 512


def _rope64(x, cos, sin_signed):
    lane = lax.broadcasted_iota(jnp.int32, x.shape, 1)
    first = (lane % MLA_ROPE) < (MLA_ROPE // 2)
    rot = jnp.where(first, pltpu.roll(x, LANES - MLA_ROPE // 2, 1), pltpu.roll(x, MLA_ROPE // 2, 1))
    return x * cos + rot * sin_signed


def _mla_prep_kernel(c_ref, pe_ref, gain_ref, w_ref, cos_ref, sin_ref,
                     kn_ref, v_ref, qpe_ref, kpe_ref):
    cn = _rmsnorm_rows(c_ref[...], gain_ref[...]).astype(BF16)
    kv = _dot(cn, w_ref[...])
    kn_ref[...] = kv[:, :GROUP].astype(BF16)
    v_ref[...] = kv[:, GROUP:].astype(BF16)
    cos = cos_ref[...]
    sin = sin_ref[...]
    lane = lax.broadcasted_iota(jnp.int32, (MLA_TM, LANES), 1)
    low = lane < MLA_ROPE
    for c in range(2):
        r = _rope64(pe_ref[:, c * LANES:(c + 1) * LANES], cos, sin)
        qpe_ref[:, (2 * c) * LANES:(2 * c + 1) * LANES] = jnp.where(low, r, 0.0).astype(BF16)
        qpe_ref[:, (2 * c + 1) * LANES:(2 * c + 2) * LANES] = jnp.where(
            low, pltpu.roll(r, MLA_ROPE, 1), 0.0).astype(BF16)
    r = _rope64(pe_ref[:, 2 * LANES:3 * LANES], cos, sin)
    kpe_ref[...] = jnp.where(low, r, 0.0).astype(BF16)


def _mla_prep(proj, gain, w_ukv, cos, sin):
    tm = MLA_TM
    row512 = pl.BlockSpec((tm, GROUP), lambda i: (i, 0))
    row128 = pl.BlockSpec((tm, LANES), lambda i: (i, 0))
    return pl.pallas_call(
        _mla_prep_kernel,
        grid=(ROWS // tm,),
        in_specs=[
            pl.BlockSpec((tm, GROUP), lambda i: (i, G_CKV)),
            pl.BlockSpec((tm, GROUP), lambda i: (i, G_PE)),
            pl.BlockSpec((1, MLA_KV_RANK), lambda i: (0, 0)),
            pl.BlockSpec((MLA_KV_RANK, 2 * GROUP), lambda i: (0, 0)),
            row128, row128,
        ],
        out_specs=[row512, row512, row512, row128],
        out_shape=[
            jax.ShapeDtypeStruct((ROWS, GROUP), BF16),
            jax.ShapeDtypeStruct((ROWS, GROUP), BF16),
            jax.ShapeDtypeStruct((ROWS, GROUP), BF16),
            jax.ShapeDtypeStruct((ROWS, LANES), BF16),
        ],
        compiler_params=_cparams(("parallel",), 32),
        name="mla_prep",
    )(proj, proj, gain, w_ukv, cos, sin)


MLA_T = 256


def _mla_kernel(qn_ref, qpe_ref, kn_ref, kpe_ref, v_ref, o_ref, kc_ref):
    i = pl.program_id(2)

    @pl.when(i == 0)
    def _():
        kc_ref[:, :LANES] = kn_ref[...]
        kc_ref[:, LANES:] = kpe_ref[...]

    q = jnp.concatenate([qn_ref[...].astype(BF16), qpe_ref[...]], axis=1)
    scale = (MLA_NOPE + MLA_ROPE) ** -0.5
    row = lax.broadcasted_iota(jnp.int32, (MLA_T, MLA_T), 0)
    col = lax.broadcasted_iota(jnp.int32, (MLA_T, MLA_T), 1)

    def scores(kidx):
        k0 = pl.multiple_of(kidx * MLA_T, MLA_T)
        s = _dot_nt(q, kc_ref[pl.ds(k0, MLA_T), :]) * scale
        return s, v_ref[pl.ds(k0, MLA_T), :]

    s, v = scores(i)
    s = jnp.where(col <= row, s, -jnp.inf)
    m = jnp.max(s, axis=-1, keepdims=True)
    p = jnp.exp(s - m)
    l = jnp.sum(p, axis=-1, keepdims=True)
    acc = _dot(p.astype(BF16), v)

    def body(t, c):
        m, l, acc = c
        s, v = scores(t)
        m_new = jnp.maximum(m, jnp.max(s, axis=-1, keepdims=True))
        alpha = jnp.exp(m - m_new)
        p = jnp.exp(s - m_new)
        l = alpha * l + jnp.sum(p, axis=-1, keepdims=True)
        acc = alpha * acc + _dot(p.astype(BF16), v)
        return m_new, l, acc

    m, l, acc = lax.fori_loop(0, i, body, (m, l, acc))
    o_ref[...] = (acc / l).astype(o_ref.dtype)


def _mla_attention(proj, kn, v, qpe, kpe):
    nq = SEQ // MLA_T
    return pl.pallas_call(
        _mla_kernel,
        grid=(BATCH, HEADS, nq),
        in_specs=[
            pl.BlockSpec((MLA_T, LANES), lambda b, h, i: (b * nq + i, G_MQN * HB + h)),
            pl.BlockSpec((MLA_T, LANES), lambda b, h, i: (b * nq + i, h)),
            pl.BlockSpec((SEQ, LANES), lambda b, h, i: (b, h)),
            pl.BlockSpec((SEQ, LANES), lambda b, h, i: (b, 0)),
            pl.BlockSpec((SEQ, LANES), lambda b, h, i: (b, h)),
        ],
        out_specs=pl.BlockSpec((MLA_T, LANES), lambda b, h, i: (b * nq + i, h)),
        out_shape=jax.ShapeDtypeStruct((ROWS, BRANCH_WIDTH), BF16),
        scratch_shapes=[pltpu.VMEM((SEQ, 2 * LANES), BF16)],
        compiler_params=_cparams(("parallel", "parallel", "arbitrary"), 32),
        name="mla_attention",
    )(proj, qpe, kn, kpe, v)


def _rope128(x, cos, sin_signed):
    return x * cos + pltpu.roll(x, HEAD_DIM // 2, 1) * sin_signed


def _ret_kernel(q_ref, k_ref, v_ref, g_ref, cos_ref, sin_ref, dec_ref, qd_ref, kd_ref, cd_ref, o_ref):
    C = RET_CHUNK
    dec = dec_ref[...]
    qd = qd_ref[...]
    kd = kd_ref[...]
    cd = cd_ref[...]
    state = jnp.zeros((HEAD_DIM, HEAD_DIM), F32)
    for n in range(SEQ // C):
        sl = pl.ds(n * C, C)
        cos = cos_ref[sl, :]
        sin = sin_ref[sl, :]
        q = _rope128(q_ref[sl, :], cos, sin)
        k = _rope128(k_ref[sl, :], cos, sin) * (HEAD_DIM ** -0.5)
        v = v_ref[sl, :].astype(BF16)
        scores = _dot_nt(q.astype(BF16), k.astype(BF16)) * dec
        inner = _dot(scores.astype(BF16), v)
        cross = _dot((q * qd).astype(BF16), state.astype(BF16))
        state = state * cd + _dot_tn((k * kd).astype(BF16), v)
        o = inner + cross
        o = o * lax.rsqrt(jnp.mean(o * o, axis=-1, keepdims=True) + EPS)
        g = g_ref[sl, :]
        o_ref[sl, :] = (o * (g * jax.nn.sigmoid(g))).astype(o_ref.dtype)


def _ret_consts():
    C = RET_CHUNK
    h = np.arange(HEADS, dtype=np.float32)
    log_gamma = jnp.log1p(-(2.0 ** (-5.0 - jnp.arange(HEADS, dtype=F32))))
    idx = jnp.arange(C, dtype=F32)
    diff = idx[:, None] - idx[None, :]
    decay = jnp.where(diff >= 0, jnp.exp(log_gamma[:, None, None] * jnp.maximum(diff, 0.0)), 0.0)
    k_decay = jnp.exp(log_gamma[:, None] * (C - 1.0 - idx)[None, :])
    q_decay = jnp.exp(log_gamma[:, None] * (idx + 1.0)[None, :])
    chunk_decay = jnp.exp(log_gamma * C)
    del h
    bc = lambda t: jnp.broadcast_to(t[:, :, None], (HEADS, C, HEAD_DIM))
    cdm = jnp.broadcast_to(chunk_decay[:, None, None], (HEADS, HEAD_DIM, HEAD_DIM))
    return decay, bc(q_decay), bc(k_decay), cdm


def _retention(proj, cos, sin):
    seq = lambda g: pl.BlockSpec((SEQ, HEAD_DIM), lambda b, h: (b, g * HB + h))
    tab = pl.BlockSpec((SEQ, LANES), lambda b, h: (b, 0))
    cst = pl.BlockSpec((None, HEAD_DIM, HEAD_DIM), lambda b, h: (h, 0, 0))
    dec, qd, kd, cd = _ret_consts()
    return pl.pallas_call(
        _ret_kernel,
        grid=(BATCH, HEADS),
        in_specs=[seq(G_RQ), seq(G_RK), seq(G_RV), seq(G_RG), tab, tab, cst, cst, cst, cst],
        out_specs=pl.BlockSpec((SEQ, HEAD_DIM), lambda b, h: (b, h)),
        out_shape=jax.ShapeDtypeStruct((ROWS, BRANCH_WIDTH), BF16),
        compiler_params=_cparams(("parallel", "parallel"), 32),
        name="retention",
    )(proj, proj, proj, proj, cos, sin, dec, qd, kd, cd)


LRU_TS = 256
LRU_PAD = 8


def _lru_kernel(x_ref, y_ref, cw_ref, cb_ref, wa_ref, ba_ref, wx_ref, bx_ref, lam_ref, o_ref,
                xp_ref, p_ref, q_ref):
    W = LRU_WIDTH
    xp_ref[0:LRU_PAD, :] = jnp.zeros((LRU_PAD, W), F32)
    cw = cw_ref[...]
    cb = cb_ref[...]
    ba = ba_ref[...]
    bx = bx_ref[...]
    nlam = -lam_ref[...]
    sp = jnp.maximum(nlam, 0.0) + jnp.log1p(jnp.exp(-jnp.abs(nlam)))
    sub = lax.broadcasted_iota(jnp.int32, (LRU_TS, W), 0) % SUBLANES

    def copy(c, _):
        r0 = pl.multiple_of(c * LRU_TS, LRU_TS)
        xp_ref[pl.ds(LRU_PAD + r0, LRU_TS), :] = x_ref[pl.ds(r0, LRU_TS), :]
        return 0

    lax.fori_loop(0, SEQ // LRU_TS, copy, 0)

    def gates(c, _):
        r0 = pl.multiple_of(c * LRU_TS, LRU_TS)
        win = xp_ref[pl.ds(r0, LRU_TS + LRU_PAD), :]
        lead = LRU_PAD - (CONV_WIDTH - 1)
        xc = cb + win[lead:lead + LRU_TS, :] * cw[0:1, :]
        for tap in range(1, CONV_WIDTH):
            xc = xc + win[lead + tap:lead + tap + LRU_TS, :] * cw[tap:tap + 1, :]
        ra, ri = [], []
        for blk in range(LRU_BLOCKS):
            xb = xc[:, blk * LRU_BLOCK:(blk + 1) * LRU_BLOCK].astype(BF16)
            ra.append(_dot(xb, wa_ref[blk]))
            ri.append(_dot(xb, wx_ref[blk]))
        r = jax.nn.sigmoid(jnp.concatenate(ra, axis=1) + ba)
        gi = jax.nn.sigmoid(jnp.concatenate(ri, axis=1) + bx)
        log_a = -LRU_C * r * sp
        a = jnp.exp(log_a)
        t = jnp.tanh(log_a)
        b = jnp.sqrt(-2.0 * t / (1.0 - t)) * (gi * xc)
        for k in (1, 2, 4):
            keep = sub >= k
            b = jnp.where(keep, b + a * pltpu.roll(b, k, 0), b)
            a = jnp.where(keep, a * pltpu.roll(a, k, 0), a)
        p_ref[pl.ds(r0, LRU_TS), :] = a
        q_ref[pl.ds(r0, LRU_TS), :] = b
        return 0

    lax.fori_loop(0, SEQ // LRU_TS, gates, 0)

    def carry(gidx, h):
        r0 = pl.multiple_of(gidx * SUBLANES, SUBLANES)
        rows = p_ref[pl.ds(r0, SUBLANES), :] * h + q_ref[pl.ds(r0, SUBLANES), :]
        q_ref[pl.ds(r0, SUBLANES), :] = rows
        return jnp.broadcast_to(rows[SUBLANES - 1:SUBLANES, :], (SUBLANES, W))

    lax.fori_loop(0, SEQ // SUBLANES, carry, jnp.zeros((SUBLANES, W), F32), unroll=8)

    def out(c, _):
        r0 = pl.multiple_of(c * LRU_TS, LRU_TS)
        y = y_ref[pl.ds(r0, LRU_TS), :]
        o_ref[pl.ds(r0, LRU_TS), :] = (q_ref[pl.ds(r0, LRU_TS), :] * jax.nn.gelu(y)).astype(o_ref.dtype)
        return 0

    lax.fori_loop(0, SEQ // LRU_TS, out, 0)


def _rglru(proj, conv_w, conv_b, w_a, b_a, w_x, b_x, lam):
    W = LRU_WIDTH
    vec = pl.BlockSpec((1, W), lambda b: (0, 0))
    blk = pl.BlockSpec((LRU_BLOCKS, LRU_BLOCK, LRU_BLOCK), lambda b: (0, 0, 0))
    return pl.pallas_call(
        _lru_kernel,
        grid=(BATCH,),
        in_specs=[
            pl.BlockSpec((SEQ, W), lambda b: (b, G_LX)),
            pl.BlockSpec((SEQ, W), lambda b: (b, G_LY)),
            pl.BlockSpec((CONV_WIDTH, W), lambda b: (0, 0)),
            vec, blk, vec, blk, vec, vec,
        ],
        out_specs=pl.BlockSpec((SEQ, W), lambda b: (b, 0)),
        out_shape=jax.ShapeDtypeStruct((ROWS, W), BF16),
        scratch_shapes=[
            pltpu.VMEM((SEQ + LRU_PAD, W), F32),
            pltpu.VMEM((SEQ, W), F32),
            pltpu.VMEM((SEQ, W), F32),
        ],
        compiler_params=_cparams(("parallel",), 48),
        name="rglru",
    )(proj, proj, conv_w, conv_b, w_a, b_a, w_x, b_x, lam)


MERGE_TM, MERGE_TN = 1024, 256


def _merge_kernel(h_ref, o0_ref, o1_ref, o2_ref, o3_ref, wg_ref, wb_ref, out_ref):
    h = h_ref[...]
    acc = None
    for br, o_ref in enumerate((o0_ref, o1_ref, o2_ref, o3_ref)):
        t = jax.nn.sigmoid(_dot(h, wg_ref[br])) * _dot(o_ref[...], wb_ref[br])
        acc = t if acc is None else acc + t
    out_ref[...] = acc.astype(out_ref.dtype)


def _merge(h, branches, w_gate, w_branch):
    tm, tn = MERGE_TM, MERGE_TN
    obs = pl.BlockSpec((tm, BRANCH_WIDTH), lambda i, j: (i, 0))
    return pl.pallas_call(
        _merge_kernel,
        grid=(ROWS // tm, D_MODEL // tn),
        in_specs=[
            pl.BlockSpec((tm, D_MODEL), lambda i, j: (i, 0)),
            obs, obs, obs, obs,
            pl.BlockSpec((N_BRANCH, D_MODEL, tn), lambda i, j: (0, 0, j)),
            pl.BlockSpec((N_BRANCH, BRANCH_WIDTH, tn), lambda i, j: (0, 0, j)),
        ],
        out_specs=pl.BlockSpec((tm, tn), lambda i, j: (i, j)),
        out_shape=jax.ShapeDtypeStruct((ROWS, D_MODEL), BF16),
        compiler_params=_cparams(("parallel", "arbitrary"), 48),
        name="gated_merge",
    )(h, *branches, w_gate, w_branch)


OUT_TM, OUT_TN = 1024, 512


def _outproj_kernel(m_ref, w_ref, x_ref, o_ref):
    o_ref[...] = x_ref[...] + _dot(m_ref[...], w_ref[...])


def _outproj(mixed, w_out, x):
    tm, tn = OUT_TM, OUT_TN
    return pl.pallas_call(
        _outproj_kernel,
        grid=(ROWS // tm, D_MODEL // tn),
        in_specs=[
            pl.BlockSpec((tm, D_MODEL), lambda i, j: (i, 0)),
            pl.BlockSpec((D_MODEL, tn), lambda i, j: (0, j)),
            pl.BlockSpec((tm, tn), lambda i, j: (i, j)),
        ],
        out_specs=pl.BlockSpec((tm, tn), lambda i, j: (i, j)),
        out_shape=jax.ShapeDtypeStruct((ROWS, D_MODEL), F32),
        compiler_params=_cparams(("parallel", "arbitrary"), 40),
        name="out_proj",
    )(mixed, w_out, x)


FFN_TM, FFN_TF = 512, 512


def _ffn_kernel(x_ref, g_ref, wg_ref, wu_ref, wd_ref, o_ref, h_ref):
    f = pl.program_id(1)

    @pl.when(f == 0)
    def _():
        _norm_into(x_ref, g_ref, h_ref, FFN_TM)

    h = h_ref[...]
    a = _dot(h, wg_ref[...])
    act = (a * jax.nn.sigmoid(a) * _dot(h, wu_ref[...])).astype(BF16)
    d = _dot(act, wd_ref[...])

    @pl.when(f == 0)
    def _():
        o_ref[...] = x_ref[...] + d

    @pl.when(f > 0)
    def _():
        o_ref[...] += d


def _ffn(x, ln, wg, wu, wd):
    tm, tf = FFN_TM, FFN_TF
    return pl.pallas_call(
        _ffn_kernel,
        grid=(ROWS // tm, D_FF // tf),
        in_specs=[
            pl.BlockSpec((tm, D_MODEL), lambda i, f: (i, 0)),
            pl.BlockSpec((1, D_MODEL), lambda i, f: (0, 0)),
            pl.BlockSpec((D_MODEL, tf), lambda i, f: (0, f)),
            pl.BlockSpec((D_MODEL, tf), lambda i, f: (0, f)),
            pl.BlockSpec((tf, D_MODEL), lambda i, f: (f, 0)),
        ],
        out_specs=pl.BlockSpec((tm, D_MODEL), lambda i, f: (i, 0)),
        out_shape=jax.ShapeDtypeStruct((ROWS, D_MODEL), F32),
        scratch_shapes=[pltpu.VMEM((tm, D_MODEL), BF16)],
        compiler_params=_cparams(("parallel", "arbitrary"), 48),
        name="ffn",
    )(x, ln, wg, wu, wd)


FINAL_TM = 512


def _final_norm_kernel(x_ref, g_ref, o_ref):
    o_ref[...] = _rmsnorm_rows(x_ref[...], g_ref[...])


def _final_norm(x, g):
    tm = FINAL_TM
    return pl.pallas_call(
        _final_norm_kernel,
        grid=(ROWS // tm,),
        in_specs=[pl.BlockSpec((tm, D_MODEL), lambda i: (i, 0)), pl.BlockSpec((1, D_MODEL), lambda i: (0, 0))],
        out_specs=pl.BlockSpec((tm, D_MODEL), lambda i: (i, 0)),
        out_shape=jax.ShapeDtypeStruct((ROWS, D_MODEL), F32),
        compiler_params=_cparams(("parallel",), 32),
        name="final_norm",
    )(x, g)


def _pack_w_in(w):
    D = w.shape[0]
    mq0 = 3 * GROUP
    mq = w[:, mq0:mq0 + HEADS * (MLA_NOPE + MLA_ROPE)].reshape(D, HEADS, MLA_NOPE + MLA_ROPE)
    ckv0 = mq0 + HEADS * (MLA_NOPE + MLA_ROPE)
    kr0 = ckv0 + MLA_KV_RANK
    rest0 = kr0 + MLA_ROPE
    pe_pad = GROUP - HEADS * MLA_ROPE - MLA_ROPE
    parts = [
        w[:, :mq0],
        mq[:, :, :MLA_NOPE].reshape(D, HEADS * MLA_NOPE),
        w[:, ckv0:kr0],
        mq[:, :, MLA_NOPE:].reshape(D, HEADS * MLA_ROPE),
        w[:, kr0:rest0],
        jnp.zeros((D, pe_pad), w.dtype),
        w[:, rest0:],
    ]
    return jnp.concatenate(parts, axis=1).astype(BF16)


def _pack_w_ukv(w):
    R = w.shape[0]
    t = w.reshape(R, HEADS, 2 * HEAD_DIM)
    return jnp.concatenate([t[:, :, :HEAD_DIM].reshape(R, GROUP), t[:, :, HEAD_DIM:].reshape(R, GROUP)],
                           axis=1).astype(BF16)


def kernel(x, positions, ln1, w_in, mla_kv_gain, mla_w_ukv, lru_conv_w, lru_conv_b, lru_w_a, lru_b_a,
           lru_w_x, lru_b_x, lru_lambda, w_branch, w_gate, w_out, ln2, ffn_w_gate, ffn_w_up,
           ffn_w_down, ln_final):
    ret_cos, ret_sin, mla_cos, mla_sin = _rope_tables(positions)
    xf = x.reshape(ROWS, D_MODEL)
    row = lambda t: t.reshape(1, -1)
    for l in range(DEPTH):
        proj, h = _proj(xf, row(ln1[l]), _pack_w_in(w_in[l]))
        o_sb = _sb_attention(proj)
        kn, v, qpe, kpe = _mla_prep(proj, row(mla_kv_gain[l]), _pack_w_ukv(mla_w_ukv[l]), mla_cos, mla_sin)
        o_mla = _mla_attention(proj, kn, v, qpe, kpe)
        o_ret = _retention(proj, ret_cos, ret_sin)
        o_lru = _rglru(proj, lru_conv_w[l], row(lru_conv_b[l]), lru_w_a[l].astype(BF16), row(lru_b_a[l]),
                       lru_w_x[l].astype(BF16), row(lru_b_x[l]), row(lru_lambda[l]))
        mixed = _merge(h, (o_sb, o_mla, o_ret, o_lru), w_gate[l].astype(BF16), w_branch[l].astype(BF16))
        xf = _outproj(mixed, w_out[l].astype(BF16), xf)
        xf = _ffn(xf, row(ln2[l]), ffn_w_gate[l].astype(BF16), ffn_w_up[l].astype(BF16),
                  ffn_w_down[l].astype(BF16))
    return _final_norm(xf, row(ln_final)).reshape(BATCH, SEQ, D_MODEL)
```

```python
import functools
import math

import numpy as np
import jax
import jax.numpy as jnp
from jax import lax
from jax.experimental import pallas as pl
from jax.experimental.pallas import tpu as pltpu

F32 = jnp.float32
BF16 = jnp.bfloat16

D_MODEL = 2048
BATCH = 8
SEQ = 2048
DEPTH = 4
ROWS = BATCH * SEQ
HEAD_DIM = 128
HEADS = 4
ROPE_THETA = 10000.0
EPS = 1e-6
N_BRANCH = 4
BRANCH_WIDTH = 512
MLA_NOPE = 128
MLA_ROPE = 64
MLA_KV_RANK = 512
RET_CHUNK = 128
LRU_WIDTH = 512
LRU_BLOCKS = 4
LRU_BLOCK = 128
CONV_WIDTH = 4
LRU_C = 8.0
D_FF = 5632

LANES = 128
SUBLANES = 8
GROUP = 512
G_SBQ, G_SBK, G_SBV, G_MQN, G_CKV, G_PE, G_RQ, G_RK, G_RV, G_RG, G_LX, G_LY = range(12)
PROJ_W = 12 * GROUP
HB = GROUP // LANES

VMEM_MB = 1024 * 1024
LOG2E = math.log2(math.e)


def _cparams(sem, vmem_mb):
    return pltpu.CompilerParams(dimension_semantics=sem, vmem_limit_bytes=vmem_mb * VMEM_MB)


def _dot(a, b):
    return jnp.dot(a, b, preferred_element_type=F32)


def _dot_nt(a, b):
    return lax.dot_general(a, b, (((1,), (1,)), ((), ())), preferred_element_type=F32)


def _dot_tn(a, b):
    return lax.dot_general(a, b, (((0,), (0,)), ((), ())), preferred_element_type=F32)


def _rmsnorm_rows(x, g):
    return x * lax.rsqrt(jnp.mean(x * x, axis=-1, keepdims=True) + EPS) * g


def _layer_vec(width, l):
    return pl.BlockSpec((None, 1, width), lambda *_: (l, 0, 0))


def _tables_kernel(pos_ref, fr_ref, sr_ref, fm_ref, sm_ref, rc_ref, rs_ref, mc_ref, ms_ref):
    pos = pos_ref[...]
    ang = pos * fr_ref[...]
    rc_ref[...] = jnp.cos(ang)
    rs_ref[...] = jnp.sin(ang) * sr_ref[...]
    ang = pos * fm_ref[...]
    mc_ref[...] = jnp.cos(ang)
    ms_ref[...] = jnp.sin(ang) * sm_ref[...]


def _rope_tables(positions):
    tm = 1024
    pos = jnp.broadcast_to(positions.reshape(ROWS, 1).astype(F32), (ROWS, LANES))

    def freq(d):
        return ROPE_THETA ** (-jnp.arange(0, d, 2, dtype=F32) / d)

    def sign(d):
        return jnp.concatenate([-jnp.ones((d // 2,), F32), jnp.ones((d // 2,), F32)])

    fr = jnp.tile(freq(HEAD_DIM), 2).reshape(1, LANES)
    sr = sign(HEAD_DIM).reshape(1, LANES)
    fm = jnp.tile(freq(MLA_ROPE), 4).reshape(1, LANES)
    sm = jnp.tile(sign(MLA_ROPE), 2).reshape(1, LANES)
    row = pl.BlockSpec((tm, LANES), lambda i: (i, 0))
    vec = pl.BlockSpec((1, LANES), lambda i: (0, 0))
    shp = jax.ShapeDtypeStruct((ROWS, LANES), F32)
    return pl.pallas_call(
        _tables_kernel,
        grid=(ROWS // tm,),
        in_specs=[row, vec, vec, vec, vec],
        out_specs=[row, row, row, row],
        out_shape=[shp, shp, shp, shp],
        compiler_params=_cparams(("parallel",), 32),
        name="rope_tables",
    )(pos, fr, sr, fm, sm)


PROJ_TM, PROJ_TN = 1024, 512
NORM_CHUNK = 256


def _norm_into(x_ref, g_ref, h_ref, rows):
    g = g_ref[...]

    def body(c, _):
        r0 = pl.multiple_of(c * NORM_CHUNK, NORM_CHUNK)
        x = x_ref[pl.ds(r0, NORM_CHUNK), :]
        h_ref[pl.ds(r0, NORM_CHUNK), :] = _rmsnorm_rows(x, g).astype(h_ref.dtype)
        return 0

    lax.fori_loop(0, rows // NORM_CHUNK, body, 0)


def _proj_kernel(x_ref, g_ref, w_ref, o_ref, h_ref):
    @pl.when(pl.program_id(1) == 0)
    def _():
        _norm_into(x_ref, g_ref, h_ref, PROJ_TM)

    o_ref[...] = _dot(h_ref[...], w_ref[...])


def _proj(x, ln, w, l):
    tm, tn = PROJ_TM, PROJ_TN
    return pl.pallas_call(
        _proj_kernel,
        grid=(ROWS // tm, PROJ_W // tn),
        in_specs=[
            pl.BlockSpec((tm, D_MODEL), lambda i, j: (i, 0)),
            _layer_vec(D_MODEL, l),
            pl.BlockSpec((None, D_MODEL, tn), lambda i, j: (l, 0, j)),
        ],
        out_specs=[
            pl.BlockSpec((tm, tn), lambda i, j: (i, j)),
            pl.BlockSpec((tm, D_MODEL), lambda i, j: (i, 0)),
        ],
        out_shape=[
            jax.ShapeDtypeStruct((ROWS, PROJ_W), F32),
            jax.ShapeDtypeStruct((ROWS, D_MODEL), BF16),
        ],
        compiler_params=_cparams(("parallel", "arbitrary"), 48),
        name="norm_proj",
    )(x, ln, w)


SB_TQ = 256
SB_TK = 128
SB_PAIR = 2
SB_GROUP = 4


def _sb_logits(q, kb_ref, k0, nblk):
    return _dot_nt(q, kb_ref[pl.ds(k0, nblk * SB_TK), :]) * (HEAD_DIM ** -0.5 * LOG2E)


def _sb_weights(z, tri, nblk, carry, masks):
    ls_l, r_l = [], []
    for j in range(nblk - 1, -1, -1):
        zj = z[:, j * SB_TK:(j + 1) * SB_TK]
        m = jnp.maximum(zj, 0.0) + jnp.log2(1.0 + jnp.exp2(-jnp.abs(zj)))
        ls_l.append(zj - m)
        if masks is not None:
            m = jnp.where(masks[j], m, 0.0)
        r_l.append(_dot(m.astype(BF16), tri))
    ws = []
    for j, ls, r in zip(range(nblk - 1, -1, -1), ls_l, r_l):
        w = jnp.exp2(ls + r[:, :SB_TK] + carry)
        if masks is not None:
            w = jnp.where(masks[j], w, 0.0)
        ws.append(w.astype(BF16))
        carry = carry + r[:, SB_TK:]
    return carry, jnp.concatenate(ws[::-1], axis=1)


def _sb_kernel(q_ref, k_ref, v_ref, tri_ref, o_ref, kb_ref, vb_ref):
    i = pl.program_id(2)

    @pl.when(i == 0)
    def _():
        kb_ref[...] = k_ref[...].astype(BF16)
        vb_ref[...] = v_ref[...].astype(BF16)

    q = q_ref[...].astype(BF16)
    tri = tri_ref[...]
    row = lax.broadcasted_iota(jnp.int32, (SB_TQ, SB_TK), 0)
    col = lax.broadcasted_iota(jnp.int32, (SB_TQ, SB_TK), 1)

    def pair(k0, carry, acc, masks):
        carry, w = _sb_weights(_sb_logits(q, kb_ref, k0, SB_PAIR), tri, SB_PAIR, carry, masks)
        return carry, acc + _dot(w, vb_ref[pl.ds(k0, SB_PAIR * SB_TK), :])

    carry = jnp.zeros((SB_TQ, SB_TK), F32)
    acc = jnp.zeros((SB_TQ, HEAD_DIM), F32)
    masks = [(col + j * SB_TK) < row for j in range(SB_PAIR)]
    carry, acc = pair(pl.multiple_of(i * SB_TQ, SB_TQ), carry, acc, masks)
    carry, acc = lax.cond(
        i % 2 == 1,
        lambda c, a: pair(pl.multiple_of((i - 1) * SB_TQ, SB_TQ), c, a, None),
        lambda c, a: (c, a),
        carry, acc)

    ngroups = i // 2
    span = SB_GROUP * SB_TK
    start = lambda g: pl.multiple_of(jnp.maximum(g, 0) * span, span)

    def body(t, c):
        carry, acc, z, w_prev = c
        g = ngroups - 1 - t
        z_next = _sb_logits(q, kb_ref, start(g - 1), SB_GROUP)
        acc = acc + _dot(w_prev, vb_ref[pl.ds(start(g + 1), span), :])
        carry, w = _sb_weights(z, tri, SB_GROUP, carry, None)
        return carry, acc, z_next, w

    init = (carry, acc, _sb_logits(q, kb_ref, start(ngroups - 1), SB_GROUP), jnp.zeros((SB_TQ, span), BF16))
    carry, acc, _, w_last = lax.fori_loop(0, ngroups, body, init)
    acc = acc + _dot(w_last, vb_ref[pl.ds(start(0), span), :])
    o_ref[...] = acc.astype(o_ref.dtype)


def _sb_tri():
    j = np.arange(SB_TK)
    t = (j[:, None] > j[None, :]).astype(np.float32)
    return jnp.asarray(-np.concatenate([t, np.ones((SB_TK, SB_TK), np.float32)], axis=1), dtype=BF16)


def _sb_attention(proj):
    nq = SEQ // SB_TQ
    return pl.pallas_call(
        _sb_kernel,
        grid=(BATCH, HEADS, nq),
        in_specs=[
            pl.BlockSpec((SB_TQ, HEAD_DIM), lambda b, h, i: (b * nq + i, G_SBQ * HB + h)),
            pl.BlockSpec((SEQ, HEAD_DIM), lambda b, h, i: (b, G_SBK * HB + h)),
            pl.BlockSpec((SEQ, HEAD_DIM), lambda b, h, i: (b, G_SBV * HB + h)),
            pl.BlockSpec((SB_TK, 2 * SB_TK), lambda b, h, i: (0, 0)),
        ],
        out_specs=pl.BlockSpec((SB_TQ, HEAD_DIM), lambda b, h, i: (b * nq + i, h)),
        out_shape=jax.ShapeDtypeStruct((ROWS, BRANCH_WIDTH), BF16),
        scratch_shapes=[pltpu.VMEM((SEQ, HEAD_DIM), BF16), pltpu.VMEM((SEQ, HEAD_DIM), BF16)],
        compiler_params=_cparams(("parallel", "parallel", "arbitrary"), 32),
        name="sb_attention",
    )(proj, proj, proj, _sb_tri())


MLA_TM = 512


def _rope64(x, cos, sin_signed):
    lane = lax.broadcasted_iota(jnp.int32, x.shape, 1)
    first = (lane % MLA_ROPE) < (MLA_ROPE // 2)
    rot = jnp.where(first, pltpu.roll(x, LANES - MLA_ROPE // 2, 1), pltpu.roll(x, MLA_ROPE // 2, 1))
    return x * cos + rot * sin_signed


def _mla_prep_kernel(c_ref, pe_ref, gain_ref, w_ref, cos_ref, sin_ref,
                     kn_ref, v_ref, qpe_ref, kpe_ref):
    cn = _rmsnorm_rows(c_ref[...], gain_ref[...]).astype(BF16)
    kv = _dot(cn, w_ref[...])
    kn_ref[...] = kv[:, :GROUP].astype(BF16)
    v_ref[...] = kv[:, GROUP:].astype(BF16)
    cos = cos_ref[...]
    sin = sin_ref[...]
    lane = lax.broadcasted_iota(jnp.int32, (MLA_TM, LANES), 1)
    low = lane < MLA_ROPE
    for c in range(2):
        r = _rope64(pe_ref[:, c * LANES:(c + 1) * LANES], cos, sin)
        qpe_ref[:, (2 * c) * LANES:(2 * c + 1) * LANES] = jnp.where(low, r, 0.0).astype(BF16)
        qpe_ref[:, (2 * c + 1) * LANES:(2 * c + 2) * LANES] = jnp.where(
            low, pltpu.roll(r, MLA_ROPE, 1), 0.0).astype(BF16)
    r = _rope64(pe_ref[:, 2 * LANES:3 * LANES], cos, sin)
    kpe_ref[...] = jnp.where(low, r, 0.0).astype(BF16)


def _mla_prep(proj, gain, w_ukv, cos, sin, l):
    tm = MLA_TM
    row512 = pl.BlockSpec((tm, GROUP), lambda i: (i, 0))
    row128 = pl.BlockSpec((tm, LANES), lambda i: (i, 0))
    return pl.pallas_call(
        _mla_prep_kernel,
        grid=(ROWS // tm,),
        in_specs=[
            pl.BlockSpec((tm, GROUP), lambda i: (i, G_CKV)),
            pl.BlockSpec((tm, GROUP), lambda i: (i, G_PE)),
            _layer_vec(MLA_KV_RANK, l),
            pl.BlockSpec((None, MLA_KV_RANK, 2 * GROUP), lambda i: (l, 0, 0)),
            row128, row128,
        ],
        out_specs=[row512, row512, row512, row128],
        out_shape=[
            jax.ShapeDtypeStruct((ROWS, GROUP), BF16),
            jax.ShapeDtypeStruct((ROWS, GROUP), BF16),
            jax.ShapeDtypeStruct((ROWS, GROUP), BF16),
            jax.ShapeDtypeStruct((ROWS, LANES), BF16),
        ],
        compiler_params=_cparams(("parallel",), 32),
        name="mla_prep",
    )(proj, proj, gain, w_ukv, cos, sin)


MLA_T = 256
MLA_GROUP = 2
MLA_SCALE = (MLA_NOPE + MLA_ROPE) ** -0.5 * LOG2E


def _mla_step(q, kc_ref, v_ref, k0, nk, m, l, acc, mask):
    s = _dot_nt(q, kc_ref[pl.ds(k0, nk), :]) * MLA_SCALE
    if mask is not None:
        s = jnp.where(mask, s, -jnp.inf)
    m_new = jnp.maximum(m, jnp.max(s, axis=-1, keepdims=True))
    alpha = jnp.exp2(m - m_new)
    p = jnp.exp2(s - m_new)
    l = alpha * l + jnp.sum(p, axis=-1, keepdims=True)
    acc = alpha * acc + _dot(p.astype(BF16), v_ref[pl.ds(k0, nk), :])
    return m_new, l, acc


def _mla_kernel(qn_ref, qpe_ref, kn_ref, kpe_ref, v_ref, o_ref, kc_ref):
    i = pl.program_id(2)

    @pl.when(i == 0)
    def _():
        kc_ref[:, :LANES] = kn_ref[...]
        kc_ref[:, LANES:] = kpe_ref[...]

    q = jnp.concatenate([qn_ref[...].astype(BF16), qpe_ref[...]], axis=1)
    step = functools.partial(_mla_step, q, kc_ref, v_ref)
    row = lax.broadcasted_iota(jnp.int32, (MLA_T, MLA_T), 0)
    col = lax.broadcasted_iota(jnp.int32, (MLA_T, MLA_T), 1)

    m = jnp.full((MLA_T, 1), -jnp.inf, F32)
    l = jnp.zeros((MLA_T, 1), F32)
    acc = jnp.zeros((MLA_T, HEAD_DIM), F32)
    m, l, acc = step(pl.multiple_of(i * MLA_T, MLA_T), MLA_T, m, l, acc, col <= row)
    m, l, acc = lax.cond(
        i % 2 == 1,
        lambda m, l, a: step(pl.multiple_of((i - 1) * MLA_T, MLA_T), MLA_T, m, l, a, None),
        lambda m, l, a: (m, l, a),
        m, l, acc)

    npairs = i // 2
    span = MLA_GROUP * MLA_T
    start = lambda g: pl.multiple_of(jnp.maximum(g, 0) * span, span)
    logits = lambda g: _dot_nt(q, kc_ref[pl.ds(start(g), span), :]) * MLA_SCALE

    def body(t, c):
        m, l, acc, s, p_prev = c
        s_next = logits(jnp.minimum(t + 1, npairs - 1))
        acc = acc + _dot(p_prev, v_ref[pl.ds(start(t - 1), span), :])
        m_new = jnp.maximum(m, jnp.max(s, axis=-1, keepdims=True))
        alpha = jnp.exp2(m - m_new)
        p = jnp.exp2(s - m_new)
        l = alpha * l + jnp.sum(p, axis=-1, keepdims=True)
        return m_new, l, alpha * acc, s_next, p.astype(BF16)

    init = (m, l, acc, logits(0), jnp.zeros((MLA_T, span), BF16))
    m, l, acc, _, p_last = lax.fori_loop(0, npairs, body, init)
    acc = acc + _dot(p_last, v_ref[pl.ds(start(npairs - 1), span), :])
    o_ref[...] = (acc / l).astype(o_ref.dtype)


def _mla_attention(proj, kn, v, qpe, kpe):
    nq = SEQ // MLA_T
    return pl.pallas_call(
        _mla_kernel,
        grid=(BATCH, HEADS, nq),
        in_specs=[
            pl.BlockSpec((MLA_T, LANES), lambda b, h, i: (b * nq + i, G_MQN * HB + h)),
            pl.BlockSpec((MLA_T, LANES), lambda b, h, i: (b * nq + i, h)),
            pl.BlockSpec((SEQ, LANES), lambda b, h, i: (b, h)),
            pl.BlockSpec((SEQ, LANES), lambda b, h, i: (b, 0)),
            pl.BlockSpec((SEQ, LANES), lambda b, h, i: (b, h)),
        ],
        out_specs=pl.BlockSpec((MLA_T, LANES), lambda b, h, i: (b * nq + i, h)),
        out_shape=jax.ShapeDtypeStruct((ROWS, BRANCH_WIDTH), BF16),
        scratch_shapes=[pltpu.VMEM((SEQ, 2 * LANES), BF16)],
        compiler_params=_cparams(("parallel", "parallel", "arbitrary"), 32),
        name="mla_attention",
    )(proj, qpe, kn, kpe, v)


def _rope128(x, cos, sin_signed):
    return x * cos + pltpu.roll(x, HEAD_DIM // 2, 1) * sin_signed


def _ret_kernel(q_ref, k_ref, v_ref, g_ref, cos_ref, sin_ref, dec_ref, qd_ref, kd_ref, cd_ref, o_ref):
    C = RET_CHUNK
    dec = dec_ref[...]
    qd = qd_ref[...]
    kd = kd_ref[...]
    cd = cd_ref[...]
    state = jnp.zeros((HEAD_DIM, HEAD_DIM), F32)
    for n in range(SEQ // C):
        sl = pl.ds(n * C, C)
        cos = cos_ref[sl, :]
        sin = sin_ref[sl, :]
        q = _rope128(q_ref[sl, :], cos, sin)
        k = _rope128(k_ref[sl, :], cos, sin) * (HEAD_DIM ** -0.5)
        v = v_ref[sl, :].astype(BF16)
        scores = _dot_nt(q.astype(BF16), k.astype(BF16)) * dec
        inner = _dot(scores.astype(BF16), v)
        cross = _dot((q * qd).astype(BF16), state.astype(BF16))
        state = state * cd + _dot_tn((k * kd).astype(BF16), v)
        o = inner + cross
        o = o * lax.rsqrt(jnp.mean(o * o, axis=-1, keepdims=True) + EPS)
        g = g_ref[sl, :]
        o_ref[sl, :] = (o * (g * jax.nn.sigmoid(g))).astype(o_ref.dtype)


def _ret_consts():
    C = RET_CHUNK
    log_gamma = jnp.log1p(-(2.0 ** (-5.0 - jnp.arange(HEADS, dtype=F32))))
    idx = jnp.arange(C, dtype=F32)
    diff = idx[:, None] - idx[None, :]
    decay = jnp.where(diff >= 0, jnp.exp(log_gamma[:, None, None] * jnp.maximum(diff, 0.0)), 0.0)
    k_decay = jnp.exp(log_gamma[:, None] * (C - 1.0 - idx)[None, :])
    q_decay = jnp.exp(log_gamma[:, None] * (idx + 1.0)[None, :])
    chunk_decay = jnp.exp(log_gamma * C)
    bc = lambda t: jnp.broadcast_to(t[:, :, None], (HEADS, C, HEAD_DIM))
    cdm = jnp.broadcast_to(chunk_decay[:, None, None], (HEADS, HEAD_DIM, HEAD_DIM))
    return decay, bc(q_decay), bc(k_decay), cdm


def _retention(proj, cos, sin, consts):
    seq = lambda g: pl.BlockSpec((SEQ, HEAD_DIM), lambda b, h: (b, g * HB + h))
    tab = pl.BlockSpec((SEQ, LANES), lambda b, h: (b, 0))
    cst = pl.BlockSpec((None, HEAD_DIM, HEAD_DIM), lambda b, h: (h, 0, 0))
    return pl.pallas_call(
        _ret_kernel,
        grid=(BATCH, HEADS),
        in_specs=[seq(G_RQ), seq(G_RK), seq(G_RV), seq(G_RG), tab, tab, cst, cst, cst, cst],
        out_specs=pl.BlockSpec((SEQ, HEAD_DIM), lambda b, h: (b, h)),
        out_shape=jax.ShapeDtypeStruct((ROWS, BRANCH_WIDTH), BF16),
        compiler_params=_cparams(("parallel", "parallel"), 32),
        name="retention",
    )(proj, proj, proj, proj, cos, sin, *consts)


LRU_TS = 256
LRU_PAD = 8


def _lru_kernel(x_ref, y_ref, cw_ref, cb_ref, wa_ref, ba_ref, wx_ref, bx_ref, lam_ref, o_ref,
                xp_ref, p_ref, q_ref):
    W = LRU_WIDTH
    xp_ref[0:LRU_PAD, :] = jnp.zeros((LRU_PAD, W), F32)
    cw = cw_ref[...]
    cb = cb_ref[...]
    ba = ba_ref[...]
    bx = bx_ref[...]
    nlam = -lam_ref[...]
    sp = jnp.maximum(nlam, 0.0) + jnp.log1p(jnp.exp(-jnp.abs(nlam)))
    sub = lax.broadcasted_iota(jnp.int32, (LRU_TS, W), 0) % SUBLANES

    def copy(c, _):
        r0 = pl.multiple_of(c * LRU_TS, LRU_TS)
        xp_ref[pl.ds(LRU_PAD + r0, LRU_TS), :] = x_ref[pl.ds(r0, LRU_TS), :]
        return 0

    lax.fori_loop(0, SEQ // LRU_TS, copy, 0)

    def gates(c, _):
        r0 = pl.multiple_of(c * LRU_TS, LRU_TS)
        win = xp_ref[pl.ds(r0, LRU_TS + LRU_PAD), :]
        lead = LRU_PAD - (CONV_WIDTH - 1)
        xc = cb + win[lead:lead + LRU_TS, :] * cw[0:1, :]
        for tap in range(1, CONV_WIDTH):
            xc = xc + win[lead + tap:lead + tap + LRU_TS, :] * cw[tap:tap + 1, :]
        ra, ri = [], []
        for blk in range(LRU_BLOCKS):
            xb = xc[:, blk * LRU_BLOCK:(blk + 1) * LRU_BLOCK].astype(BF16)
            ra.append(_dot(xb, wa_ref[blk]))
            ri.append(_dot(xb, wx_ref[blk]))
        r = jax.nn.sigmoid(jnp.concatenate(ra, axis=1) + ba)
        gi = jax.nn.sigmoid(jnp.concatenate(ri, axis=1) + bx)
        log_a = -LRU_C * r * sp
        a = jnp.exp(log_a)
        t = jnp.tanh(log_a)
        b = jnp.sqrt(-2.0 * t / (1.0 - t)) * (gi * xc)
        for k in (1, 2, 4):
            keep = sub >= k
            b = jnp.where(keep, b + a * pltpu.roll(b, k, 0), b)
            a = jnp.where(keep, a * pltpu.roll(a, k, 0), a)
        p_ref[pl.ds(r0, LRU_TS), :] = a
        q_ref[pl.ds(r0, LRU_TS), :] = b
        return 0

    lax.fori_loop(0, SEQ // LRU_TS, gates, 0)

    def carry(gidx, h):
        r0 = pl.multiple_of(gidx * SUBLANES, SUBLANES)
        rows = p_ref[pl.ds(r0, SUBLANES), :] * h + q_ref[pl.ds(r0, SUBLANES), :]
        q_ref[pl.ds(r0, SUBLANES), :] = rows
        return jnp.broadcast_to(rows[SUBLANES - 1:SUBLANES, :], (SUBLANES, W))

    lax.fori_loop(0, SEQ // SUBLANES, carry, jnp.zeros((SUBLANES, W), F32), unroll=8)

    def out(c, _):
        r0 = pl.multiple_of(c * LRU_TS, LRU_TS)
        y = y_ref[pl.ds(r0, LRU_TS), :]
        o_ref[pl.ds(r0, LRU_TS), :] = (q_ref[pl.ds(r0, LRU_TS), :] * jax.nn.gelu(y)).astype(o_ref.dtype)
        return 0

    lax.fori_loop(0, SEQ // LRU_TS, out, 0)


def _rglru(proj, conv_w, conv_b, w_a, b_a, w_x, b_x, lam, l):
    W = LRU_WIDTH
    vec = _layer_vec(W, l)
    blk = pl.BlockSpec((None, LRU_BLOCKS, LRU_BLOCK, LRU_BLOCK), lambda b: (l, 0, 0, 0))
    return pl.pallas_call(
        _lru_kernel,
        grid=(BATCH,),
        in_specs=[
            pl.BlockSpec((SEQ, W), lambda b: (b, G_LX)),
            pl.BlockSpec((SEQ, W), lambda b: (b, G_LY)),
            pl.BlockSpec((None, CONV_WIDTH, W), lambda b: (l, 0, 0)),
            vec, blk, vec, blk, vec, vec,
        ],
        out_specs=pl.BlockSpec((SEQ, W), lambda b: (b, 0)),
        out_shape=jax.ShapeDtypeStruct((ROWS, W), BF16),
        scratch_shapes=[
            pltpu.VMEM((SEQ + LRU_PAD, W), F32),
            pltpu.VMEM((SEQ, W), F32),
            pltpu.VMEM((SEQ, W), F32),
        ],
        compiler_params=_cparams(("parallel",), 48),
        name="rglru",
    )(proj, proj, conv_w, conv_b, w_a, b_a, w_x, b_x, lam)


MERGE_TM, MERGE_TN = 1024, 256


def _merge_kernel(h_ref, o0_ref, o1_ref, o2_ref, o3_ref, wg_ref, wb_ref, out_ref):
    h = h_ref[...]
    acc = None
    for br, o_ref in enumerate((o0_ref, o1_ref, o2_ref, o3_ref)):
        t = jax.nn.sigmoid(_dot(h, wg_ref[br])) * _dot(o_ref[...], wb_ref[br])
        acc = t if acc is None else acc + t
    out_ref[...] = acc.astype(out_ref.dtype)


def _merge(h, branches, w_gate, w_branch, l):
    tm, tn = MERGE_TM, MERGE_TN
    obs = pl.BlockSpec((tm, BRANCH_WIDTH), lambda i, j: (i, 0))
    return pl.pallas_call(
        _merge_kernel,
        grid=(ROWS // tm, D_MODEL // tn),
        in_specs=[
            pl.BlockSpec((tm, D_MODEL), lambda i, j: (i, 0)),
            obs, obs, obs, obs,
            pl.BlockSpec((None, N_BRANCH, D_MODEL, tn), lambda i, j: (l, 0, 0, j)),
            pl.BlockSpec((None, N_BRANCH, BRANCH_WIDTH, tn), lambda i, j: (l, 0, 0, j)),
        ],
        out_specs=pl.BlockSpec((tm, tn), lambda i, j: (i, j)),
        out_shape=jax.ShapeDtypeStruct((ROWS, D_MODEL), BF16),
        compiler_params=_cparams(("parallel", "arbitrary"), 48),
        name="gated_merge",
    )(h, *branches, w_gate, w_branch)


OUT_TM, OUT_TN = 1024, 512


def _outproj_kernel(m_ref, w_ref, x_ref, o_ref):
    o_ref[...] = x_ref[...] + _dot(m_ref[...], w_ref[...])


def _outproj(mixed, w_out, x, l):
    tm, tn = OUT_TM, OUT_TN
    return pl.pallas_call(
        _outproj_kernel,
        grid=(ROWS // tm, D_MODEL // tn),
        in_specs=[
            pl.BlockSpec((tm, D_MODEL), lambda i, j: (i, 0)),
            pl.BlockSpec((None, D_MODEL, tn), lambda i, j: (l, 0, j)),
            pl.BlockSpec((tm, tn), lambda i, j: (i, j)),
        ],
        out_specs=pl.BlockSpec((tm, tn), lambda i, j: (i, j)),
        out_shape=jax.ShapeDtypeStruct((ROWS, D_MODEL), F32),
        compiler_params=_cparams(("parallel", "arbitrary"), 40),
        name="out_proj",
    )(mixed, w_out, x)


FFN_TM, FFN_TF = 512, 512


def _ffn_kernel(x_ref, g_ref, wg_ref, wu_ref, wd_ref, *rest, final):
    if final:
        gf_ref, o_ref, h_ref = rest
    else:
        o_ref, h_ref = rest
    f = pl.program_id(1)

    @pl.when(f == 0)
    def _():
        _norm_into(x_ref, g_ref, h_ref, FFN_TM)
        o_ref[...] = x_ref[...]

    h = h_ref[...]
    a = _dot(h, wg_ref[...])
    act = (a * jax.nn.sigmoid(a) * _dot(h, wu_ref[...])).astype(BF16)
    o_ref[...] += _dot(act, wd_ref[...])

    if final:
        @pl.when(f == pl.num_programs(1) - 1)
        def _():
            _norm_into(o_ref, gf_ref, o_ref, FFN_TM)


def _ffn(x, ln, wg, wu, wd, l, ln_final=None):
    tm, tf = FFN_TM, FFN_TF
    final = ln_final is not None
    in_specs = [
        pl.BlockSpec((tm, D_MODEL), lambda i, f: (i, 0)),
        _layer_vec(D_MODEL, l),
        pl.BlockSpec((None, D_MODEL, tf), lambda i, f: (l, 0, f)),
        pl.BlockSpec((None, D_MODEL, tf), lambda i, f: (l, 0, f)),
        pl.BlockSpec((None, tf, D_MODEL), lambda i, f: (l, f, 0)),
    ]
    args = [x, ln, wg, wu, wd]
    if final:
        in_specs.append(pl.BlockSpec((1, D_MODEL), lambda i, f: (0, 0)))
        args.append(ln_final)
    return pl.pallas_call(
        functools.partial(_ffn_kernel, final=final),
        grid=(ROWS // tm, D_FF // tf),
        in_specs=in_specs,
        out_specs=pl.BlockSpec((tm, D_MODEL), lambda i, f: (i, 0)),
        out_shape=jax.ShapeDtypeStruct((ROWS, D_MODEL), F32),
        scratch_shapes=[pltpu.VMEM((tm, D_MODEL), BF16)],
        compiler_params=_cparams(("parallel", "arbitrary"), 48),
        name="ffn_final" if final else "ffn",
    )(*args)


def _pack_w_in(w):
    lead = w.shape[:-1]
    mq0 = 3 * GROUP
    mq = w[..., mq0:mq0 + HEADS * (MLA_NOPE + MLA_ROPE)].reshape(*lead, HEADS, MLA_NOPE + MLA_ROPE)
    ckv0 = mq0 + HEADS * (MLA_NOPE + MLA_ROPE)
    kr0 = ckv0 + MLA_KV_RANK
    rest0 = kr0 + MLA_ROPE
    pe_pad = GROUP - HEADS * MLA_ROPE - MLA_ROPE
    parts = [
        w[..., :mq0],
        mq[..., :MLA_NOPE].reshape(*lead, HEADS * MLA_NOPE),
        w[..., ckv0:kr0],
        mq[..., MLA_NOPE:].reshape(*lead, HEADS * MLA_ROPE),
        w[..., kr0:rest0],
        jnp.zeros((*lead, pe_pad), w.dtype),
        w[..., rest0:],
    ]
    return jnp.concatenate(parts, axis=-1).astype(BF16)


def _pack_w_ukv(w):
    lead = w.shape[:-1]
    t = w.reshape(*lead, HEADS, 2 * HEAD_DIM)
    return jnp.concatenate([t[..., :HEAD_DIM].reshape(*lead, GROUP), t[..., HEAD_DIM:].reshape(*lead, GROUP)],
                           axis=-1).astype(BF16)


def kernel(x, positions, ln1, w_in, mla_kv_gain, mla_w_ukv, lru_conv_w, lru_conv_b, lru_w_a, lru_b_a,
           lru_w_x, lru_b_x, lru_lambda, w_branch, w_gate, w_out, ln2, ffn_w_gate, ffn_w_up,
           ffn_w_down, ln_final):
    ret_cos, ret_sin, mla_cos, mla_sin = _rope_tables(positions)
    ret_consts = _ret_consts()
    vec = lambda t: t.reshape(DEPTH, 1, -1)
    w_in_p = _pack_w_in(w_in)
    w_ukv_p = _pack_w_ukv(mla_w_ukv)
    w_a, w_x = lru_w_a.astype(BF16), lru_w_x.astype(BF16)
    w_gate_b, w_branch_b, w_out_b = w_gate.astype(BF16), w_branch.astype(BF16), w_out.astype(BF16)
    wf_gate, wf_up, wf_down = ffn_w_gate.astype(BF16), ffn_w_up.astype(BF16), ffn_w_down.astype(BF16)
    ln1, ln2, kv_gain = vec(ln1), vec(ln2), vec(mla_kv_gain)
    conv_b, b_a, b_x, lam = vec(lru_conv_b), vec(lru_b_a), vec(lru_b_x), vec(lru_lambda)

    xf = x.reshape(ROWS, D_MODEL)
    for l in range(DEPTH):
        proj, h = _proj(xf, ln1, w_in_p, l)
        o_sb = _sb_attention(proj)
        kn, v, qpe, kpe = _mla_prep(proj, kv_gain, w_ukv_p, mla_cos, mla_sin, l)
        o_mla = _mla_attention(proj, kn, v, qpe, kpe)
        o_ret = _retention(proj, ret_cos, ret_sin, ret_consts)
        o_lru = _rglru(proj, lru_conv_w, conv_b, w_a, b_a, w_x, b_x, lam, l)
        mixed = _merge(h, (o_sb, o_mla, o_ret, o_lru), w_gate_b, w_branch_b, l)
        xf = _outproj(mixed, w_out_b, xf, l)
        xf = _ffn(xf, ln2, wf_gate, wf_up, wf_down, l,
                  ln_final.reshape(1, D_MODEL) if l == DEPTH - 1 else None)
    return xf.reshape(BATCH, SEQ, D_MODEL)
```

```python
import functools
import math

import numpy as np
import jax
import jax.numpy as jnp
from jax import lax
from jax.experimental import pallas as pl
from jax.experimental.pallas import tpu as pltpu

F32 = jnp.float32
BF16 = jnp.bfloat16

D_MODEL = 2048
BATCH = 8
SEQ = 2048
DEPTH = 4
ROWS = BATCH * SEQ
HEAD_DIM = 128
HEADS = 4
ROPE_THETA = 10000.0
EPS = 1e-6
N_BRANCH = 4
BRANCH_WIDTH = 512
MLA_NOPE = 128
MLA_ROPE = 64
MLA_KV_RANK = 512
RET_CHUNK = 128
LRU_WIDTH = 512
LRU_BLOCKS = 4
LRU_BLOCK = 128
CONV_WIDTH = 4
LRU_C = 8.0
D_FF = 5632

LANES = 128
SUBLANES = 8
GROUP = 512
G_SBQ, G_SBK, G_SBV, G_MQN, G_CKV, G_PE, G_RQ, G_RK, G_RV, G_RG, G_LX, G_LY = range(12)
PROJ_W = 12 * GROUP
HB = GROUP // LANES

VMEM_MB = 1024 * 1024
LOG2E = math.log2(math.e)


def _cparams(sem, vmem_mb):
    return pltpu.CompilerParams(dimension_semantics=sem, vmem_limit_bytes=vmem_mb * VMEM_MB)


def _dot(a, b):
    return jnp.dot(a, b, preferred_element_type=F32)


def _dot_nt(a, b):
    return lax.dot_general(a, b, (((1,), (1,)), ((), ())), preferred_element_type=F32)


def _dot_tn(a, b):
    return lax.dot_general(a, b, (((0,), (0,)), ((), ())), preferred_element_type=F32)


def _rmsnorm_rows(x, g):
    return x * lax.rsqrt(jnp.mean(x * x, axis=-1, keepdims=True) + EPS) * g


def _layer_vec(width, l):
    return pl.BlockSpec((None, 1, width), lambda *_: (l, 0, 0))


def _tables_kernel(pos_ref, fr_ref, sr_ref, fm_ref, sm_ref, rc_ref, rs_ref, mc_ref, ms_ref):
    pos = pos_ref[...]
    ang = pos * fr_ref[...]
    rc_ref[...] = jnp.cos(ang)
    rs_ref[...] = jnp.sin(ang) * sr_ref[...]
    ang = pos * fm_ref[...]
    mc_ref[...] = jnp.cos(ang)
    ms_ref[...] = jnp.sin(ang) * sm_ref[...]


def _rope_tables(positions):
    tm = 1024
    pos = jnp.broadcast_to(positions.reshape(ROWS, 1).astype(F32), (ROWS, LANES))

    def freq(d):
        return ROPE_THETA ** (-jnp.arange(0, d, 2, dtype=F32) / d)

    def sign(d):
        return jnp.concatenate([-jnp.ones((d // 2,), F32), jnp.ones((d // 2,), F32)])

    fr = jnp.tile(freq(HEAD_DIM), 2).reshape(1, LANES)
    sr = sign(HEAD_DIM).reshape(1, LANES)
    fm = jnp.tile(freq(MLA_ROPE), 4).reshape(1, LANES)
    sm = jnp.tile(sign(MLA_ROPE), 2).reshape(1, LANES)
    row = pl.BlockSpec((tm, LANES), lambda i: (i, 0))
    vec = pl.BlockSpec((1, LANES), lambda i: (0, 0))
    shp = jax.ShapeDtypeStruct((ROWS, LANES), F32)
    return pl.pallas_call(
        _tables_kernel,
        grid=(ROWS // tm,),
        in_specs=[row, vec, vec, vec, vec],
        out_specs=[row, row, row, row],
        out_shape=[shp, shp, shp, shp],
        compiler_params=_cparams(("parallel",), 32),
        name="rope_tables",
    )(pos, fr, sr, fm, sm)


NORM_TM = 512
NORM_CHUNK = 256


def _norm_into(x_ref, g_ref, h_ref, rows):
    g = g_ref[...]

    def body(c, _):
        r0 = pl.multiple_of(c * NORM_CHUNK, NORM_CHUNK)
        x = x_ref[pl.ds(r0, NORM_CHUNK), :]
        h_ref[pl.ds(r0, NORM_CHUNK), :] = _rmsnorm_rows(x, g).astype(h_ref.dtype)
        return 0

    lax.fori_loop(0, rows // NORM_CHUNK, body, 0)


def _norm_kernel(x_ref, g_ref, h_ref):
    _norm_into(x_ref, g_ref, h_ref, NORM_TM)


def _norm(x, ln, l):
    tm = NORM_TM
    return pl.pallas_call(
        _norm_kernel,
        grid=(ROWS // tm,),
        in_specs=[pl.BlockSpec((tm, D_MODEL), lambda i: (i, 0)), _layer_vec(D_MODEL, l)],
        out_specs=pl.BlockSpec((tm, D_MODEL), lambda i: (i, 0)),
        out_shape=jax.ShapeDtypeStruct((ROWS, D_MODEL), BF16),
        compiler_params=_cparams(("parallel",), 32),
        name="input_norm",
    )(x, ln)


PROJ_TM, PROJ_TN = 2048, 512


def _proj_kernel(h_ref, w_ref, o_ref):
    o_ref[...] = _dot(h_ref[...], w_ref[...])


def _proj(h, w, l):
    tm, tn = PROJ_TM, PROJ_TN
    return pl.pallas_call(
        _proj_kernel,
        grid=(ROWS // tm, PROJ_W // tn),
        in_specs=[
            pl.BlockSpec((tm, D_MODEL), lambda i, j: (i, 0)),
            pl.BlockSpec((None, D_MODEL, tn), lambda i, j: (l, 0, j)),
        ],
        out_specs=pl.BlockSpec((tm, tn), lambda i, j: (i, j)),
        out_shape=jax.ShapeDtypeStruct((ROWS, PROJ_W), F32),
        compiler_params=_cparams(("parallel", "arbitrary"), 40),
        name="in_proj",
    )(h, w)


SB_TK = 128
SB_GROUP = 4
SB_TQ = SB_GROUP * SB_TK


def _sb_logits(q, kb_ref, k0, nblk):
    return _dot_nt(q, kb_ref[pl.ds(k0, nblk * SB_TK), :]) * (HEAD_DIM ** -0.5 * LOG2E)


def _sb_weights(z, tri, nblk, carry, masks):
    ls_l, r_l = [], []
    for j in range(nblk - 1, -1, -1):
        zj = z[:, j * SB_TK:(j + 1) * SB_TK]
        m = jnp.maximum(zj, 0.0) + jnp.log2(1.0 + jnp.exp2(-jnp.abs(zj)))
        ls_l.append(zj - m)
        if masks is not None:
            m = jnp.where(masks[j], m, 0.0)
        r_l.append(_dot(m.astype(BF16), tri))
    ws = []
    for j, ls, r in zip(range(nblk - 1, -1, -1), ls_l, r_l):
        w = jnp.exp2(ls + r[:, :SB_TK] + carry)
        if masks is not None:
            w = jnp.where(masks[j], w, 0.0)
        ws.append(w.astype(BF16))
        carry = carry + r[:, SB_TK:]
    return carry, jnp.concatenate(ws[::-1], axis=1)


def _sb_kernel(q_ref, k_ref, v_ref, tri_ref, o_ref, kb_ref, vb_ref):
    i = pl.program_id(2)

    @pl.when(i == 0)
    def _():
        kb_ref[...] = k_ref[...].astype(BF16)
        vb_ref[...] = v_ref[...].astype(BF16)

    q = q_ref[...].astype(BF16)
    tri = tri_ref[...]
    row = lax.broadcasted_iota(jnp.int32, (SB_TQ, SB_TK), 0)
    col = lax.broadcasted_iota(jnp.int32, (SB_TQ, SB_TK), 1)

    def group(g, carry, acc, masks):
        k0 = pl.multiple_of(g * SB_TQ, SB_TQ)
        carry, w = _sb_weights(_sb_logits(q, kb_ref, k0, SB_GROUP), tri, SB_GROUP, carry, masks)
        return carry, acc + _dot(w, vb_ref[pl.ds(k0, SB_TQ), :])

    masks = [(col + j * SB_TK) < row for j in range(SB_GROUP)]
    carry, acc = group(i, jnp.zeros((SB_TQ, SB_TK), F32), jnp.zeros((SB_TQ, HEAD_DIM), F32), masks)
    carry, acc = lax.fori_loop(0, i, lambda t, c: group(i - 1 - t, c[0], c[1], None), (carry, acc))
    o_ref[...] = acc.astype(o_ref.dtype)


def _sb_tri():
    j = np.arange(SB_TK)
    t = (j[:, None] > j[None, :]).astype(np.float32)
    return jnp.asarray(-np.concatenate([t, np.ones((SB_TK, SB_TK), np.float32)], axis=1), dtype=BF16)


def _sb_attention(proj):
    nq = SEQ // SB_TQ
    return pl.pallas_call(
        _sb_kernel,
        grid=(BATCH, HEADS, nq),
        in_specs=[
            pl.BlockSpec((SB_TQ, HEAD_DIM), lambda b, h, i: (b * nq + i, G_SBQ * HB + h)),
            pl.BlockSpec((SEQ, HEAD_DIM), lambda b, h, i: (b, G_SBK * HB + h)),
            pl.BlockSpec((SEQ, HEAD_DIM), lambda b, h, i: (b, G_SBV * HB + h)),
            pl.BlockSpec((SB_TK, 2 * SB_TK), lambda b, h, i: (0, 0)),
        ],
        out_specs=pl.BlockSpec((SB_TQ, HEAD_DIM), lambda b, h, i: (b * nq + i, h)),
        out_shape=jax.ShapeDtypeStruct((ROWS, BRANCH_WIDTH), BF16),
        scratch_shapes=[pltpu.VMEM((SEQ, HEAD_DIM), BF16), pltpu.VMEM((SEQ, HEAD_DIM), BF16)],
        compiler_params=_cparams(("parallel", "parallel", "arbitrary"), 32),
        name="sb_attention",
    )(proj, proj, proj, _sb_tri())


MLA_TM = 512


def _rope64(x, cos, sin_signed):
    lane = lax.broadcasted_iota(jnp.int32, x.shape, 1)
    first = (lane % MLA_ROPE) < (MLA_ROPE // 2)
    rot = jnp.where(first, pltpu.roll(x, LANES - MLA_ROPE // 2, 1), pltpu.roll(x, MLA_ROPE // 2, 1))
    return x * cos + rot * sin_signed


def _mla_prep_kernel(c_ref, pe_ref, gain_ref, w_ref, cos_ref, sin_ref,
                     kn_ref, v_ref, qpe_ref, kpe_ref):
    cn = _rmsnorm_rows(c_ref[...], gain_ref[...]).astype(BF16)
    kv = _dot(cn, w_ref[...])
    kn_ref[...] = kv[:, :GROUP].astype(BF16)
    v_ref[...] = kv[:, GROUP:].astype(BF16)
    cos = cos_ref[...]
    sin = sin_ref[...]
    lane = lax.broadcasted_iota(jnp.int32, (MLA_TM, LANES), 1)
    low = lane < MLA_ROPE
    for c in range(2):
        r = _rope64(pe_ref[:, c * LANES:(c + 1) * LANES], cos, sin)
        qpe_ref[:, (2 * c) * LANES:(2 * c + 1) * LANES] = jnp.where(low, r, 0.0).astype(BF16)
        qpe_ref[:, (2 * c + 1) * LANES:(2 * c + 2) * LANES] = jnp.where(
            low, pltpu.roll(r, MLA_ROPE, 1), 0.0).astype(BF16)
    r = _rope64(pe_ref[:, 2 * LANES:3 * LANES], cos, sin)
    kpe_ref[...] = jnp.where(low, r, 0.0).astype(BF16)


def _mla_prep(proj, gain, w_ukv, cos, sin, l):
    tm = MLA_TM
    row512 = pl.BlockSpec((tm, GROUP), lambda i: (i, 0))
    row128 = pl.BlockSpec((tm, LANES), lambda i: (i, 0))
    return pl.pallas_call(
        _mla_prep_kernel,
        grid=(ROWS // tm,),
        in_specs=[
            pl.BlockSpec((tm, GROUP), lambda i: (i, G_CKV)),
            pl.BlockSpec((tm, GROUP), lambda i: (i, G_PE)),
            _layer_vec(MLA_KV_RANK, l),
            pl.BlockSpec((None, MLA_KV_RANK, 2 * GROUP), lambda i: (l, 0, 0)),
            row128, row128,
        ],
        out_specs=[row512, row512, row512, row128],
        out_shape=[
            jax.ShapeDtypeStruct((ROWS, GROUP), BF16),
            jax.ShapeDtypeStruct((ROWS, GROUP), BF16),
            jax.ShapeDtypeStruct((ROWS, GROUP), BF16),
            jax.ShapeDtypeStruct((ROWS, LANES), BF16),
        ],
        compiler_params=_cparams(("parallel",), 32),
        name="mla_prep",
    )(proj, proj, gain, w_ukv, cos, sin)


MLA_T = 512
MLA_SCALE = (MLA_NOPE + MLA_ROPE) ** -0.5 * LOG2E


def _mla_step(q, kc_ref, v_ref, g, m, l, acc, mask):
    k0 = pl.multiple_of(g * MLA_T, MLA_T)
    nk = MLA_T
    s = _dot_nt(q, kc_ref[pl.ds(k0, nk), :]) * MLA_SCALE
    if mask is not None:
        s = jnp.where(mask, s, -jnp.inf)
    m_new = jnp.maximum(m, jnp.max(s, axis=-1, keepdims=True))
    alpha = jnp.exp2(m - m_new)
    p = jnp.exp2(s - m_new)
    l = alpha * l + jnp.sum(p, axis=-1, keepdims=True)
    acc = alpha * acc + _dot(p.astype(BF16), v_ref[pl.ds(k0, nk), :])
    return m_new, l, acc


def _mla_kernel(qn_ref, qpe_ref, kn_ref, kpe_ref, v_ref, o_ref, kc_ref):
    i = pl.program_id(2)

    @pl.when(i == 0)
    def _():
        kc_ref[:, :LANES] = kn_ref[...]
        kc_ref[:, LANES:] = kpe_ref[...]

    q = jnp.concatenate([qn_ref[...].astype(BF16), qpe_ref[...]], axis=1)
    step = functools.partial(_mla_step, q, kc_ref, v_ref)
    row = lax.broadcasted_iota(jnp.int32, (MLA_T, MLA_T), 0)
    col = lax.broadcasted_iota(jnp.int32, (MLA_T, MLA_T), 1)

    m = jnp.full((MLA_T, 1), -jnp.inf, F32)
    l = jnp.zeros((MLA_T, 1), F32)
    acc = jnp.zeros((MLA_T, HEAD_DIM), F32)
    m, l, acc = step(i, m, l, acc, col <= row)
    m, l, acc = lax.fori_loop(0, i, lambda t, c: step(t, c[0], c[1], c[2], None), (m, l, acc))
    o_ref[...] = (acc / l).astype(o_ref.dtype)


def _mla_attention(proj, kn, v, qpe, kpe):
    nq = SEQ // MLA_T
    return pl.pallas_call(
        _mla_kernel,
        grid=(BATCH, HEADS, nq),
        in_specs=[
            pl.BlockSpec((MLA_T, LANES), lambda b, h, i: (b * nq + i, G_MQN * HB + h)),
            pl.BlockSpec((MLA_T, LANES), lambda b, h, i: (b * nq + i, h)),
            pl.BlockSpec((SEQ, LANES), lambda b, h, i: (b, h)),
            pl.BlockSpec((SEQ, LANES), lambda b, h, i: (b, 0)),
            pl.BlockSpec((SEQ, LANES), lambda b, h, i: (b, h)),
        ],
        out_specs=pl.BlockSpec((MLA_T, LANES), lambda b, h, i: (b * nq + i, h)),
        out_shape=jax.ShapeDtypeStruct((ROWS, BRANCH_WIDTH), BF16),
        scratch_shapes=[pltpu.VMEM((SEQ, 2 * LANES), BF16)],
        compiler_params=_cparams(("parallel", "parallel", "arbitrary"), 32),
        name="mla_attention",
    )(proj, qpe, kn, kpe, v)


def _rope128(x, cos, sin_signed):
    return x * cos + pltpu.roll(x, HEAD_DIM // 2, 1) * sin_signed


def _ret_kernel(q_ref, k_ref, v_ref, g_ref, cos_ref, sin_ref, dec_ref, qd_ref, kd_ref, cd_ref, o_ref):
    C = RET_CHUNK
    dec = dec_ref[...]
    qd = qd_ref[...]
    kd = kd_ref[...]
    cd = cd_ref[...]
    state = jnp.zeros((HEAD_DIM, HEAD_DIM), F32)
    for n in range(SEQ // C):
        sl = pl.ds(n * C, C)
        cos = cos_ref[sl, :]
        sin = sin_ref[sl, :]
        q = _rope128(q_ref[sl, :], cos, sin)
        k = _rope128(k_ref[sl, :], cos, sin) * (HEAD_DIM ** -0.5)
        v = v_ref[sl, :].astype(BF16)
        scores = _dot_nt(q.astype(BF16), k.astype(BF16)) * dec
        inner = _dot(scores.astype(BF16), v)
        cross = _dot((q * qd).astype(BF16), state.astype(BF16))
        state = state * cd + _dot_tn((k * kd).astype(BF16), v)
        o = inner + cross
        o = o * lax.rsqrt(jnp.mean(o * o, axis=-1, keepdims=True) + EPS)
        g = g_ref[sl, :]
        o_ref[sl, :] = (o * (g * jax.nn.sigmoid(g))).astype(o_ref.dtype)


def _ret_consts():
    C = RET_CHUNK
    log_gamma = jnp.log1p(-(2.0 ** (-5.0 - jnp.arange(HEADS, dtype=F32))))
    idx = jnp.arange(C, dtype=F32)
    diff = idx[:, None] - idx[None, :]
    decay = jnp.where(diff >= 0, jnp.exp(log_gamma[:, None, None] * jnp.maximum(diff, 0.0)), 0.0)
    k_decay = jnp.exp(log_gamma[:, None] * (C - 1.0 - idx)[None, :])
    q_decay = jnp.exp(log_gamma[:, None] * (idx + 1.0)[None, :])
    chunk_decay = jnp.exp(log_gamma * C)
    bc = lambda t: jnp.broadcast_to(t[:, :, None], (HEADS, C, HEAD_DIM))
    cdm = jnp.broadcast_to(chunk_decay[:, None, None], (HEADS, HEAD_DIM, HEAD_DIM))
    return decay, bc(q_decay), bc(k_decay), cdm


def _retention(proj, cos, sin, consts):
    seq = lambda g: pl.BlockSpec((SEQ, HEAD_DIM), lambda b, h: (b, g * HB + h))
    tab = pl.BlockSpec((SEQ, LANES), lambda b, h: (b, 0))
    cst = pl.BlockSpec((None, HEAD_DIM, HEAD_DIM), lambda b, h: (h, 0, 0))
    return pl.pallas_call(
        _ret_kernel,
        grid=(BATCH, HEADS),
        in_specs=[seq(G_RQ), seq(G_RK), seq(G_RV), seq(G_RG), tab, tab, cst, cst, cst, cst],
        out_specs=pl.BlockSpec((SEQ, HEAD_DIM), lambda b, h: (b, h)),
        out_shape=jax.ShapeDtypeStruct((ROWS, BRANCH_WIDTH), BF16),
        compiler_params=_cparams(("parallel", "parallel"), 32),
        name="retention",
    )(proj, proj, proj, proj, cos, sin, *consts)


LRU_TS = 256
LRU_PAD = 8


def _lru_kernel(x_ref, y_ref, cw_ref, cb_ref, wa_ref, ba_ref, wx_ref, bx_ref, lam_ref, o_ref,
                xp_ref, p_ref, q_ref):
    W = LRU_WIDTH
    xp_ref[0:LRU_PAD, :] = jnp.zeros((LRU_PAD, W), F32)
    cw = cw_ref[...]
    cb = cb_ref[...]
    ba = ba_ref[...]
    bx = bx_ref[...]
    nlam = -lam_ref[...]
    sp = jnp.maximum(nlam, 0.0) + jnp.log1p(jnp.exp(-jnp.abs(nlam)))
    sub = lax.broadcasted_iota(jnp.int32, (LRU_TS // SUBLANES, SUBLANES, W), 1)

    def copy(c, _):
        r0 = pl.multiple_of(c * LRU_TS, LRU_TS)
        xp_ref[pl.ds(LRU_PAD + r0, LRU_TS), :] = x_ref[pl.ds(r0, LRU_TS), :]
        return 0

    lax.fori_loop(0, SEQ // LRU_TS, copy, 0)

    def gates(c, _):
        r0 = pl.multiple_of(c * LRU_TS, LRU_TS)
        win = xp_ref[pl.ds(r0, LRU_TS + LRU_PAD), :]
        lead = LRU_PAD - (CONV_WIDTH - 1)
        xc = cb + win[lead:lead + LRU_TS, :] * cw[0:1, :]
        for tap in range(1, CONV_WIDTH):
            xc = xc + win[lead + tap:lead + tap + LRU_TS, :] * cw[tap:tap + 1, :]
        ra, ri = [], []
        for blk in range(LRU_BLOCKS):
            xb = xc[:, blk * LRU_BLOCK:(blk + 1) * LRU_BLOCK].astype(BF16)
            ra.append(_dot(xb, wa_ref[blk]))
            ri.append(_dot(xb, wx_ref[blk]))
        r = jax.nn.sigmoid(jnp.concatenate(ra, axis=1) + ba)
        gi = jax.nn.sigmoid(jnp.concatenate(ri, axis=1) + bx)
        log_a = -LRU_C * r * sp
        a = jnp.exp(log_a)
        t = jnp.tanh(log_a)
        b = jnp.sqrt(-2.0 * t / (1.0 - t)) * (gi * xc)
        a = a.reshape(LRU_TS // SUBLANES, SUBLANES, W)
        b = b.reshape(LRU_TS // SUBLANES, SUBLANES, W)
        for k in (1, 2, 4):
            keep = sub >= k
            b = jnp.where(keep, b + a * pltpu.roll(b, k, 1), b)
            a = jnp.where(keep, a * pltpu.roll(a, k, 1), a)
        p_ref[pl.ds(r0, LRU_TS), :] = a.reshape(LRU_TS, W)
        q_ref[pl.ds(r0, LRU_TS), :] = b.reshape(LRU_TS, W)
        return 0

    lax.fori_loop(0, SEQ // LRU_TS, gates, 0)

    def carry(gidx, h):
        r0 = pl.multiple_of(gidx * SUBLANES, SUBLANES)
        rows = p_ref[pl.ds(r0, SUBLANES), :] * h + q_ref[pl.ds(r0, SUBLANES), :]
        q_ref[pl.ds(r0, SUBLANES), :] = rows
        return jnp.broadcast_to(rows[SUBLANES - 1:SUBLANES, :], (SUBLANES, W))

    lax.fori_loop(0, SEQ // SUBLANES, carry, jnp.zeros((SUBLANES, W), F32), unroll=8)

    def out(c, _):
        r0 = pl.multiple_of(c * LRU_TS, LRU_TS)
        y = y_ref[pl.ds(r0, LRU_TS), :]
        o_ref[pl.ds(r0, LRU_TS), :] = (q_ref[pl.ds(r0, LRU_TS), :] * jax.nn.gelu(y)).astype(o_ref.dtype)
        return 0

    lax.fori_loop(0, SEQ // LRU_TS, out, 0)


def _rglru(proj, conv_w, conv_b, w_a, b_a, w_x, b_x, lam, l):
    W = LRU_WIDTH
    vec = _layer_vec(W, l)
    blk = pl.BlockSpec((None, LRU_BLOCKS, LRU_BLOCK, LRU_BLOCK), lambda b: (l, 0, 0, 0))
    return pl.pallas_call(
        _lru_kernel,
        grid=(BATCH,),
        in_specs=[
            pl.BlockSpec((SEQ, W), lambda b: (b, G_LX)),
            pl.BlockSpec((SEQ, W), lambda b: (b, G_LY)),
            pl.BlockSpec((None, CONV_WIDTH, W), lambda b: (l, 0, 0)),
            vec, blk, vec, blk, vec, vec,
        ],
        out_specs=pl.BlockSpec((SEQ, W), lambda b: (b, 0)),
        out_shape=jax.ShapeDtypeStruct((ROWS, W), BF16),
        scratch_shapes=[
            pltpu.VMEM((SEQ + LRU_PAD, W), F32),
            pltpu.VMEM((SEQ, W), F32),
            pltpu.VMEM((SEQ, W), F32),
        ],
        compiler_params=_cparams(("parallel",), 48),
        name="rglru",
    )(proj, proj, conv_w, conv_b, w_a, b_a, w_x, b_x, lam)


MERGE_TM, MERGE_TN = 1024, 256


def _merge_kernel(h_ref, o0_ref, o1_ref, o2_ref, o3_ref, wg_ref, wb_ref, out_ref):
    h = h_ref[...]
    acc = None
    for br, o_ref in enumerate((o0_ref, o1_ref, o2_ref, o3_ref)):
        t = jax.nn.sigmoid(_dot(h, wg_ref[br])) * _dot(o_ref[...], wb_ref[br])
        acc = t if acc is None else acc + t
    out_ref[...] = acc.astype(out_ref.dtype)


def _merge(h, branches, w_gate, w_branch, l):
    tm, tn = MERGE_TM, MERGE_TN
    obs = pl.BlockSpec((tm, BRANCH_WIDTH), lambda i, j: (i, 0))
    return pl.pallas_call(
        _merge_kernel,
        grid=(ROWS // tm, D_MODEL // tn),
        in_specs=[
            pl.BlockSpec((tm, D_MODEL), lambda i, j: (i, 0)),
            obs, obs, obs, obs,
            pl.BlockSpec((None, N_BRANCH, D_MODEL, tn), lambda i, j: (l, 0, 0, j)),
            pl.BlockSpec((None, N_BRANCH, BRANCH_WIDTH, tn), lambda i, j: (l, 0, 0, j)),
        ],
        out_specs=pl.BlockSpec((tm, tn), lambda i, j: (i, j)),
        out_shape=jax.ShapeDtypeStruct((ROWS, D_MODEL), BF16),
        compiler_params=_cparams(("parallel", "arbitrary"), 48),
        name="gated_merge",
    )(h, *branches, w_gate, w_branch)


OUT_TM, OUT_TN = 1024, 512


def _outproj_kernel(m_ref, w_ref, x_ref, o_ref):
    o_ref[...] = x_ref[...] + _dot(m_ref[...], w_ref[...])


def _outproj(mixed, w_out, x, l):
    tm, tn = OUT_TM, OUT_TN
    return pl.pallas_call(
        _outproj_kernel,
        grid=(ROWS // tm, D_MODEL // tn),
        in_specs=[
            pl.BlockSpec((tm, D_MODEL), lambda i, j: (i, 0)),
            pl.BlockSpec((None, D_MODEL, tn), lambda i, j: (l, 0, j)),
            pl.BlockSpec((tm, tn), lambda i, j: (i, j)),
        ],
        out_specs=pl.BlockSpec((tm, tn), lambda i, j: (i, j)),
        out_shape=jax.ShapeDtypeStruct((ROWS, D_MODEL), F32),
        compiler_params=_cparams(("parallel", "arbitrary"), 40),
        name="out_proj",
    )(mixed, w_out, x)


FFN_TM, FFN_TF = 512, 512


def _ffn_kernel(x_ref, g_ref, wg_ref, wu_ref, wd_ref, gn_ref, *rest, final):
    if final:
        o_ref, h_ref = rest
        n_ref = o_ref
    else:
        o_ref, n_ref, h_ref = rest
    f = pl.program_id(1)

    @pl.when(f == 0)
    def _():
        _norm_into(x_ref, g_ref, h_ref, FFN_TM)
        o_ref[...] = x_ref[...]

    h = h_ref[...]
    a = _dot(h, wg_ref[...])
    act = (a * jax.nn.sigmoid(a) * _dot(h, wu_ref[...])).astype(BF16)
    o_ref[...] += _dot(act, wd_ref[...])

    @pl.when(f == pl.num_programs(1) - 1)
    def _():
        _norm_into(o_ref, gn_ref, n_ref, FFN_TM)


def _ffn(x, ln, wg, wu, wd, l, ln_next, l_next, final):
    tm, tf = FFN_TM, FFN_TF
    row = pl.BlockSpec((tm, D_MODEL), lambda i, f: (i, 0))
    out_f32 = jax.ShapeDtypeStruct((ROWS, D_MODEL), F32)
    return pl.pallas_call(
        functools.partial(_ffn_kernel, final=final),
        grid=(ROWS // tm, D_FF // tf),
        in_specs=[
            row,
            _layer_vec(D_MODEL, l),
            pl.BlockSpec((None, D_MODEL, tf), lambda i, f: (l, 0, f)),
            pl.BlockSpec((None, D_MODEL, tf), lambda i, f: (l, 0, f)),
            pl.BlockSpec((None, tf, D_MODEL), lambda i, f: (l, f, 0)),
            _layer_vec(D_MODEL, l_next),
        ],
        out_specs=row if final else [row, row],
        out_shape=out_f32 if final else [out_f32, jax.ShapeDtypeStruct((ROWS, D_MODEL), BF16)],
        scratch_shapes=[pltpu.VMEM((tm, D_MODEL), BF16)],
        compiler_params=_cparams(("parallel", "arbitrary"), 52),
        name="ffn_final" if final else "ffn",
    )(x, ln, wg, wu, wd, ln_next)


def _pack_w_in(w):
    lead = w.shape[:-1]
    mq0 = 3 * GROUP
    mq = w[..., mq0:mq0 + HEADS * (MLA_NOPE + MLA_ROPE)].reshape(*lead, HEADS, MLA_NOPE + MLA_ROPE)
    ckv0 = mq0 + HEADS * (MLA_NOPE + MLA_ROPE)
    kr0 = ckv0 + MLA_KV_RANK
    rest0 = kr0 + MLA_ROPE
    pe_pad = GROUP - HEADS * MLA_ROPE - MLA_ROPE
    parts = [
        w[..., :mq0],
        mq[..., :MLA_NOPE].reshape(*lead, HEADS * MLA_NOPE),
        w[..., ckv0:kr0],
        mq[..., MLA_NOPE:].reshape(*lead, HEADS * MLA_ROPE),
        w[..., kr0:rest0],
        jnp.zeros((*lead, pe_pad), w.dtype),
        w[..., rest0:],
    ]
    return jnp.concatenate(parts, axis=-1).astype(BF16)


def _pack_w_ukv(w):
    lead = w.shape[:-1]
    t = w.reshape(*lead, HEADS, 2 * HEAD_DIM)
    return jnp.concatenate([t[..., :HEAD_DIM].reshape(*lead, GROUP), t[..., HEAD_DIM:].reshape(*lead, GROUP)],
                           axis=-1).astype(BF16)


def kernel(x, positions, ln1, w_in, mla_kv_gain, mla_w_ukv, lru_conv_w, lru_conv_b, lru_w_a, lru_b_a,
           lru_w_x, lru_b_x, lru_lambda, w_branch, w_gate, w_out, ln2, ffn_w_gate, ffn_w_up,
           ffn_w_down, ln_final):
    ret_cos, ret_sin, mla_cos, mla_sin = _rope_tables(positions)
    ret_consts = _ret_consts()
    vec = lambda t: t.reshape(DEPTH, 1, -1)
    w_in_p = _pack_w_in(w_in)
    w_ukv_p = _pack_w_ukv(mla_w_ukv)
    w_a, w_x = lru_w_a.astype(BF16), lru_w_x.astype(BF16)
    w_gate_b, w_branch_b, w_out_b = w_gate.astype(BF16), w_branch.astype(BF16), w_out.astype(BF16)
    wf_gate, wf_up, wf_down = ffn_w_gate.astype(BF16), ffn_w_up.astype(BF16), ffn_w_down.astype(BF16)
    ln1, ln2, kv_gain = vec(ln1), vec(ln2), vec(mla_kv_gain)
    conv_b, b_a, b_x, lam = vec(lru_conv_b), vec(lru_b_a), vec(lru_b_x), vec(lru_lambda)

    xf = x.reshape(ROWS, D_MODEL)
    h = _norm(xf, ln1, 0)
    for l in range(DEPTH):
        proj = _proj(h, w_in_p, l)
        o_sb = _sb_attention(proj)
        kn, v, qpe, kpe = _mla_prep(proj, kv_gain, w_ukv_p, mla_cos, mla_sin, l)
        o_mla = _mla_attention(proj, kn, v, qpe, kpe)
        o_ret = _retention(proj, ret_cos, ret_sin, ret_consts)
        o_lru = _rglru(proj, lru_conv_w, conv_b, w_a, b_a, w_x, b_x, lam, l)
        mixed = _merge(h, (o_sb, o_mla, o_ret, o_lru), w_gate_b, w_branch_b, l)
        xf = _outproj(mixed, w_out_b, xf, l)
        if l < DEPTH - 1:
            xf, h = _ffn(xf, ln2, wf_gate, wf_up, wf_down, l, ln1, l + 1, final=False)
        else:
            xf = _ffn(xf, ln2, wf_gate, wf_up, wf_down, l, ln_final.reshape(1, 1, D_MODEL), 0, final=True)
    return xf.reshape(BATCH, SEQ, D_MODEL)
```

```python
import functools
import math

import numpy as np
import jax
import jax.numpy as jnp
from jax import lax
from jax.experimental import pallas as pl
from jax.experimental.pallas import tpu as pltpu

F32 = jnp.float32
BF16 = jnp.bfloat16

D_MODEL = 2048
BATCH = 8
SEQ = 2048
DEPTH = 4
ROWS = BATCH * SEQ
HEAD_DIM = 128
HEADS = 4
ROPE_THETA = 10000.0
EPS = 1e-6
N_BRANCH = 4
BRANCH_WIDTH = 512
MLA_NOPE = 128
MLA_ROPE = 64
MLA_KV_RANK = 512
RET_CHUNK = 128
LRU_WIDTH = 512
LRU_BLOCKS = 4
LRU_BLOCK = 128
CONV_WIDTH = 4
LRU_C = 8.0
D_FF = 5632

LANES = 128
SUBLANES = 8
GROUP = 512
G_SBQ, G_SBK, G_SBV, G_MQN, G_CKV, G_PE, G_RQ, G_RK, G_RV, G_RG, G_LX, G_LY = range(12)
PROJ_W = 12 * GROUP
HB = GROUP // LANES

VMEM_MB = 1024 * 1024
LOG2E = math.log2(math.e)


def _cparams(sem, vmem_mb):
    return pltpu.CompilerParams(dimension_semantics=sem, vmem_limit_bytes=vmem_mb * VMEM_MB)


def _dot(a, b):
    return jnp.dot(a, b, preferred_element_type=F32)


def _dot_nt(a, b):
    return lax.dot_general(a, b, (((1,), (1,)), ((), ())), preferred_element_type=F32)


def _dot_tn(a, b):
    return lax.dot_general(a, b, (((0,), (0,)), ((), ())), preferred_element_type=F32)


def _rmsnorm_rows(x, g):
    return x * lax.rsqrt(jnp.mean(x * x, axis=-1, keepdims=True) + EPS) * g


def _layer_vec(width, l):
    return pl.BlockSpec((None, 1, width), lambda *_: (l, 0, 0))


def _tables_kernel(pos_ref, fr_ref, sr_ref, fm_ref, sm_ref, rc_ref, rs_ref, mc_ref, ms_ref):
    pos = pos_ref[...]
    ang = pos * fr_ref[...]
    rc_ref[...] = jnp.cos(ang)
    rs_ref[...] = jnp.sin(ang) * sr_ref[...]
    ang = pos * fm_ref[...]
    mc_ref[...] = jnp.cos(ang)
    ms_ref[...] = jnp.sin(ang) * sm_ref[...]


def _rope_tables(positions):
    tm = 1024
    pos = jnp.broadcast_to(positions.reshape(ROWS, 1).astype(F32), (ROWS, LANES))

    def freq(d):
        return ROPE_THETA ** (-jnp.arange(0, d, 2, dtype=F32) / d)

    def sign(d):
        return jnp.concatenate([-jnp.ones((d // 2,), F32), jnp.ones((d // 2,), F32)])

    fr = jnp.tile(freq(HEAD_DIM), 2).reshape(1, LANES)
    sr = sign(HEAD_DIM).reshape(1, LANES)
    fm = jnp.tile(freq(MLA_ROPE), 4).reshape(1, LANES)
    sm = jnp.tile(sign(MLA_ROPE), 2).reshape(1, LANES)
    row = pl.BlockSpec((tm, LANES), lambda i: (i, 0))
    vec = pl.BlockSpec((1, LANES), lambda i: (0, 0))
    shp = jax.ShapeDtypeStruct((ROWS, LANES), F32)
    return pl.pallas_call(
        _tables_kernel,
        grid=(ROWS // tm,),
        in_specs=[row, vec, vec, vec, vec],
        out_specs=[row, row, row, row],
        out_shape=[shp, shp, shp, shp],
        compiler_params=_cparams(("parallel",), 32),
        name="rope_tables",
    )(pos, fr, sr, fm, sm)


NORM_TM = 512
NORM_CHUNK = 256


def _norm_into(x_ref, g_ref, h_ref, rows):
    g = g_ref[...]

    def body(c, _):
        r0 = pl.multiple_of(c * NORM_CHUNK, NORM_CHUNK)
        x = x_ref[pl.ds(r0, NORM_CHUNK), :]
        h_ref[pl.ds(r0, NORM_CHUNK), :] = _rmsnorm_rows(x, g).astype(h_ref.dtype)
        return 0

    lax.fori_loop(0, rows // NORM_CHUNK, body, 0)


def _norm_kernel(x_ref, g_ref, h_ref):
    _norm_into(x_ref, g_ref, h_ref, NORM_TM)


def _norm(x, ln, l):
    tm = NORM_TM
    return pl.pallas_call(
        _norm_kernel,
        grid=(ROWS // tm,),
        in_specs=[pl.BlockSpec((tm, D_MODEL), lambda i: (i, 0)), _layer_vec(D_MODEL, l)],
        out_specs=pl.BlockSpec((tm, D_MODEL), lambda i: (i, 0)),
        out_shape=jax.ShapeDtypeStruct((ROWS, D_MODEL), BF16),
        compiler_params=_cparams(("parallel",), 32),
        name="input_norm",
    )(x, ln)


PROJ_TM, PROJ_TN = 2048, 1024


def _proj_kernel(h_ref, w_ref, o_ref):
    o_ref[...] = _dot(h_ref[...], w_ref[...])


def _proj(h, w, l):
    tm, tn = PROJ_TM, PROJ_TN
    return pl.pallas_call(
        _proj_kernel,
        grid=(ROWS // tm, PROJ_W // tn),
        in_specs=[
            pl.BlockSpec((tm, D_MODEL), lambda i, j: (i, 0)),
            pl.BlockSpec((None, D_MODEL, tn), lambda i, j: (l, 0, j)),
        ],
        out_specs=pl.BlockSpec((tm, tn), lambda i, j: (i, j)),
        out_shape=jax.ShapeDtypeStruct((ROWS, PROJ_W), F32),
        compiler_params=_cparams(("parallel", "arbitrary"), 52),
        name="in_proj",
    )(h, w)


ATT_HEADS = 4
SB_TK = 128
SB_GROUP = 4
SB_TQ = SB_GROUP * SB_TK


def _sb_weights(z, tri, nblk, carry, masks):
    ls_l, r_l = [], []
    for j in range(nblk - 1, -1, -1):
        zj = z[:, j * SB_TK:(j + 1) * SB_TK]
        m = jnp.maximum(zj, 0.0) + jnp.log2(1.0 + jnp.exp2(-jnp.abs(zj)))
        ls_l.append(zj - m)
        if masks is not None:
            m = jnp.where(masks[j], m, 0.0)
        r_l.append(_dot(m.astype(BF16), tri))
    ws = []
    for j, ls, r in zip(range(nblk - 1, -1, -1), ls_l, r_l):
        w = jnp.exp2(ls + r[:, :SB_TK] + carry)
        if masks is not None:
            w = jnp.where(masks[j], w, 0.0)
        ws.append(w.astype(BF16))
        carry = carry + r[:, SB_TK:]
    return carry, jnp.concatenate(ws[::-1], axis=1)


def _sb_kernel(q_ref, k_ref, v_ref, tri_ref, o_ref, kb_ref, vb_ref):
    i = pl.program_id(2)

    @pl.when(i == 0)
    def _():
        kb_ref[...] = k_ref[...].astype(BF16)
        vb_ref[...] = v_ref[...].astype(BF16)

    tri = tri_ref[...]
    row = lax.broadcasted_iota(jnp.int32, (SB_TQ, SB_TK), 0)
    col = lax.broadcasted_iota(jnp.int32, (SB_TQ, SB_TK), 1)
    heads = [slice(hh * HEAD_DIM, (hh + 1) * HEAD_DIM) for hh in range(ATT_HEADS)]
    qs = [q_ref[:, hd].astype(BF16) for hd in heads]

    def group(g, state, masks):
        k0 = pl.multiple_of(g * SB_TQ, SB_TQ)
        out = []
        for q, hd, (carry, acc) in zip(qs, heads, state):
            z = _dot_nt(q, kb_ref[pl.ds(k0, SB_TQ), hd]) * (HEAD_DIM ** -0.5 * LOG2E)
            carry, w = _sb_weights(z, tri, SB_GROUP, carry, masks)
            out.append((carry, acc + _dot(w, vb_ref[pl.ds(k0, SB_TQ), hd])))
        return tuple(out)

    masks = [(col + j * SB_TK) < row for j in range(SB_GROUP)]
    zero = (jnp.zeros((SB_TQ, SB_TK), F32), jnp.zeros((SB_TQ, HEAD_DIM), F32))
    state = group(i, (zero,) * ATT_HEADS, masks)
    state = lax.fori_loop(0, i, lambda t, s: group(i - 1 - t, s, None), state)
    for hd, (_, acc) in zip(heads, state):
        o_ref[:, hd] = acc.astype(o_ref.dtype)


def _sb_tri():
    j = np.arange(SB_TK)
    t = (j[:, None] > j[None, :]).astype(np.float32)
    return jnp.asarray(-np.concatenate([t, np.ones((SB_TK, SB_TK), np.float32)], axis=1), dtype=BF16)


def _sb_attention(proj):
    nq = SEQ // SB_TQ
    hw = ATT_HEADS * HEAD_DIM
    hs = HEADS // ATT_HEADS
    return pl.pallas_call(
        _sb_kernel,
        grid=(BATCH, hs, nq),
        in_specs=[
            pl.BlockSpec((SB_TQ, hw), lambda b, h, i: (b * nq + i, G_SBQ * hs + h)),
            pl.BlockSpec((SEQ, hw), lambda b, h, i: (b, G_SBK * hs + h)),
            pl.BlockSpec((SEQ, hw), lambda b, h, i: (b, G_SBV * hs + h)),
            pl.BlockSpec((SB_TK, 2 * SB_TK), lambda b, h, i: (0, 0)),
        ],
        out_specs=pl.BlockSpec((SB_TQ, hw), lambda b, h, i: (b * nq + i, h)),
        out_shape=jax.ShapeDtypeStruct((ROWS, BRANCH_WIDTH), BF16),
        scratch_shapes=[pltpu.VMEM((SEQ, hw), BF16), pltpu.VMEM((SEQ, hw), BF16)],
        compiler_params=_cparams(("parallel", "parallel", "arbitrary"), 32),
        name="sb_attention",
    )(proj, proj, proj, _sb_tri())


MLA_TM = 512


def _rope64(x, cos, sin_signed):
    lane = lax.broadcasted_iota(jnp.int32, x.shape, 1)
    first = (lane % MLA_ROPE) < (MLA_ROPE // 2)
    rot = jnp.where(first, pltpu.roll(x, LANES - MLA_ROPE // 2, 1), pltpu.roll(x, MLA_ROPE // 2, 1))
    return x * cos + rot * sin_signed


def _mla_prep_kernel(c_ref, pe_ref, gain_ref, w_ref, cos_ref, sin_ref,
                     kn_ref, v_ref, qpe_ref, kpe_ref):
    cn = _rmsnorm_rows(c_ref[...], gain_ref[...]).astype(BF16)
    kv = _dot(cn, w_ref[...])
    kn_ref[...] = kv[:, :GROUP].astype(BF16)
    v_ref[...] = kv[:, GROUP:].astype(BF16)
    cos = cos_ref[...]
    sin = sin_ref[...]
    lane = lax.broadcasted_iota(jnp.int32, (MLA_TM, LANES), 1)
    low = lane < MLA_ROPE
    for c in range(2):
        r = _rope64(pe_ref[:, c * LANES:(c + 1) * LANES], cos, sin)
        qpe_ref[:, (2 * c) * LANES:(2 * c + 1) * LANES] = jnp.where(low, r, 0.0).astype(BF16)
        qpe_ref[:, (2 * c + 1) * LANES:(2 * c + 2) * LANES] = jnp.where(
            low, pltpu.roll(r, MLA_ROPE, 1), 0.0).astype(BF16)
    r = _rope64(pe_ref[:, 2 * LANES:3 * LANES], cos, sin)
    kpe_ref[...] = jnp.where(low, r, 0.0).astype(BF16)


def _mla_prep(proj, gain, w_ukv, cos, sin, l):
    tm = MLA_TM
    row512 = pl.BlockSpec((tm, GROUP), lambda i: (i, 0))
    row128 = pl.BlockSpec((tm, LANES), lambda i: (i, 0))
    return pl.pallas_call(
        _mla_prep_kernel,
        grid=(ROWS // tm,),
        in_specs=[
            pl.BlockSpec((tm, GROUP), lambda i: (i, G_CKV)),
            pl.BlockSpec((tm, GROUP), lambda i: (i, G_PE)),
            _layer_vec(MLA_KV_RANK, l),
            pl.BlockSpec((None, MLA_KV_RANK, 2 * GROUP), lambda i: (l, 0, 0)),
            row128, row128,
        ],
        out_specs=[row512, row512, row512, row128],
        out_shape=[
            jax.ShapeDtypeStruct((ROWS, GROUP), BF16),
            jax.ShapeDtypeStruct((ROWS, GROUP), BF16),
            jax.ShapeDtypeStruct((ROWS, GROUP), BF16),
            jax.ShapeDtypeStruct((ROWS, LANES), BF16),
        ],
        compiler_params=_cparams(("parallel",), 32),
        name="mla_prep",
    )(proj, proj, gain, w_ukv, cos, sin)


MLA_T = 512
MLA_SCALE = (MLA_NOPE + MLA_ROPE) ** -0.5 * LOG2E


def _mla_step(q, k, v, m, l, acc, mask):
    s = _dot_nt(q, k) * MLA_SCALE
    if mask is not None:
        s = jnp.where(mask, s, -jnp.inf)
    m_new = jnp.maximum(m, jnp.max(s, axis=-1, keepdims=True))
    alpha = jnp.exp2(m - m_new)
    p = jnp.exp2(s - m_new)
    l = alpha * l + jnp.sum(p, axis=-1, keepdims=True)
    acc = alpha * acc + _dot(p.astype(BF16), v)
    return m_new, l, acc


def _mla_kernel(qn_ref, qpe_ref, kn_ref, kpe_ref, v_ref, o_ref, kc_ref):
    i = pl.program_id(2)
    heads = [slice(hh * HEAD_DIM, (hh + 1) * HEAD_DIM) for hh in range(ATT_HEADS)]

    @pl.when(i == 0)
    def _():
        for hh, hd in enumerate(heads):
            kc_ref[hh, :, :LANES] = kn_ref[:, hd]
            kc_ref[hh, :, LANES:] = kpe_ref[...]

    qs = [jnp.concatenate([qn_ref[:, hd].astype(BF16), qpe_ref[:, hd]], axis=1) for hd in heads]
    row = lax.broadcasted_iota(jnp.int32, (MLA_T, MLA_T), 0)
    col = lax.broadcasted_iota(jnp.int32, (MLA_T, MLA_T), 1)

    def block(g, state, mask):
        k0 = pl.multiple_of(g * MLA_T, MLA_T)
        return tuple(
            _mla_step(q, kc_ref[hh, pl.ds(k0, MLA_T), :], v_ref[pl.ds(k0, MLA_T), hd], m, l, acc, mask)
            for hh, (q, hd, (m, l, acc)) in enumerate(zip(qs, heads, state)))

    init = (jnp.full((MLA_T, 1), -jnp.inf, F32), jnp.zeros((MLA_T, 1), F32), jnp.zeros((MLA_T, HEAD_DIM), F32))
    state = block(i, (init,) * ATT_HEADS, col <= row)
    state = lax.fori_loop(0, i, lambda t, s: block(t, s, None), state)
    for hd, (_, l, acc) in zip(heads, state):
        o_ref[:, hd] = (acc / l).astype(o_ref.dtype)


def _mla_attention(proj, kn, v, qpe, kpe):
    nq = SEQ // MLA_T
    hw = ATT_HEADS * HEAD_DIM
    hs = HEADS // ATT_HEADS
    return pl.pallas_call(
        _mla_kernel,
        grid=(BATCH, hs, nq),
        in_specs=[
            pl.BlockSpec((MLA_T, hw), lambda b, h, i: (b * nq + i, G_MQN * hs + h)),
            pl.BlockSpec((MLA_T, hw), lambda b, h, i: (b * nq + i, h)),
            pl.BlockSpec((SEQ, hw), lambda b, h, i: (b, h)),
            pl.BlockSpec((SEQ, LANES), lambda b, h, i: (b, 0)),
            pl.BlockSpec((SEQ, hw), lambda b, h, i: (b, h)),
        ],
        out_specs=pl.BlockSpec((MLA_T, hw), lambda b, h, i: (b * nq + i, h)),
        out_shape=jax.ShapeDtypeStruct((ROWS, BRANCH_WIDTH), BF16),
        scratch_shapes=[pltpu.VMEM((ATT_HEADS, SEQ, 2 * LANES), BF16)],
        compiler_params=_cparams(("parallel", "parallel", "arbitrary"), 32),
        name="mla_attention",
    )(proj, qpe, kn, kpe, v)


def _rope128(x, cos, sin_signed):
    return x * cos + pltpu.roll(x, HEAD_DIM // 2, 1) * sin_signed


def _ret_kernel(q_ref, k_ref, v_ref, g_ref, cos_ref, sin_ref, dec_ref, qd_ref, kd_ref, cd_ref, o_ref):
    C = RET_CHUNK
    dec = dec_ref[...]
    qd = qd_ref[...]
    kd = kd_ref[...]
    cd = cd_ref[...]
    state = jnp.zeros((HEAD_DIM, HEAD_DIM), F32)
    for n in range(SEQ // C):
        sl = pl.ds(n * C, C)
        cos = cos_ref[sl, :]
        sin = sin_ref[sl, :]
        q = _rope128(q_ref[sl, :], cos, sin)
        k = _rope128(k_ref[sl, :], cos, sin) * (HEAD_DIM ** -0.5)
        v = v_ref[sl, :].astype(BF16)
        scores = _dot_nt(q.astype(BF16), k.astype(BF16)) * dec
        inner = _dot(scores.astype(BF16), v)
        cross = _dot((q * qd).astype(BF16), state.astype(BF16))
        state = state * cd + _dot_tn((k * kd).astype(BF16), v)
        o = inner + cross
        o = o * lax.rsqrt(jnp.mean(o * o, axis=-1, keepdims=True) + EPS)
        g = g_ref[sl, :]
        o_ref[sl, :] = (o * (g * jax.nn.sigmoid(g))).astype(o_ref.dtype)


def _ret_consts():
    C = RET_CHUNK
    log_gamma = jnp.log1p(-(2.0 ** (-5.0 - jnp.arange(HEADS, dtype=F32))))
    idx = jnp.arange(C, dtype=F32)
    diff = idx[:, None] - idx[None, :]
    decay = jnp.where(diff >= 0, jnp.exp(log_gamma[:, None, None] * jnp.maximum(diff, 0.0)), 0.0)
    k_decay = jnp.exp(log_gamma[:, None] * (C - 1.0 - idx)[None, :])
    q_decay = jnp.exp(log_gamma[:, None] * (idx + 1.0)[None, :])
    chunk_decay = jnp.exp(log_gamma * C)
    bc = lambda t: jnp.broadcast_to(t[:, :, None], (HEADS, C, HEAD_DIM))
    cdm = jnp.broadcast_to(chunk_decay[:, None, None], (HEADS, HEAD_DIM, HEAD_DIM))
    return decay, bc(q_decay), bc(k_decay), cdm


def _retention(proj, cos, sin, consts):
    seq = lambda g: pl.BlockSpec((SEQ, HEAD_DIM), lambda b, h: (b, g * HB + h))
    tab = pl.BlockSpec((SEQ, LANES), lambda b, h: (b, 0))
    cst = pl.BlockSpec((None, HEAD_DIM, HEAD_DIM), lambda b, h: (h, 0, 0))
    return pl.pallas_call(
        _ret_kernel,
        grid=(BATCH, HEADS),
        in_specs=[seq(G_RQ), seq(G_RK), seq(G_RV), seq(G_RG), tab, tab, cst, cst, cst, cst],
        out_specs=pl.BlockSpec((SEQ, HEAD_DIM), lambda b, h: (b, h)),
        out_shape=jax.ShapeDtypeStruct((ROWS, BRANCH_WIDTH), BF16),
        compiler_params=_cparams(("parallel", "parallel"), 32),
        name="retention",
    )(proj, proj, proj, proj, cos, sin, *consts)


LRU_TS = 256
LRU_PAD = 8


def _lru_kernel(x_ref, y_ref, cw_ref, cb_ref, wa_ref, ba_ref, wx_ref, bx_ref, lam_ref, o_ref,
                xp_ref, p_ref, q_ref):
    W = LRU_WIDTH
    xp_ref[0:LRU_PAD, :] = jnp.zeros((LRU_PAD, W), F32)
    cw = cw_ref[...]
    cb = cb_ref[...]
    ba = ba_ref[...]
    bx = bx_ref[...]
    nlam = -lam_ref[...]
    sp = jnp.maximum(nlam, 0.0) + jnp.log1p(jnp.exp(-jnp.abs(nlam)))
    sub = lax.broadcasted_iota(jnp.int32, (LRU_TS // SUBLANES, SUBLANES, W), 1)

    def copy(c, _):
        r0 = pl.multiple_of(c * LRU_TS, LRU_TS)
        xp_ref[pl.ds(LRU_PAD + r0, LRU_TS), :] = x_ref[pl.ds(r0, LRU_TS), :]
        return 0

    lax.fori_loop(0, SEQ // LRU_TS, copy, 0)

    def gates(c, _):
        r0 = pl.multiple_of(c * LRU_TS, LRU_TS)
        win = xp_ref[pl.ds(r0, LRU_TS + LRU_PAD), :].reshape(LRU_TS // SUBLANES + 1, SUBLANES, W)
        xc = cb
        for tap in range(CONV_WIDTH):
            s = CONV_WIDTH - 1 - tap
            rolled = pltpu.roll(win, s, 1) if s else win
            shifted = jnp.where(sub >= s, rolled[1:], rolled[:-1]) if s else win[1:]
            xc = xc + shifted * cw[tap:tap + 1, :]
        xc = xc.reshape(LRU_TS, W)
        ra, ri = [], []
        for blk in range(LRU_BLOCKS):
            xb = xc[:, blk * LRU_BLOCK:(blk + 1) * LRU_BLOCK].astype(BF16)
            ra.append(_dot(xb, wa_ref[blk]))
            ri.append(_dot(xb, wx_ref[blk]))
        r = jax.nn.sigmoid(jnp.concatenate(ra, axis=1) + ba)
        gi = jax.nn.sigmoid(jnp.concatenate(ri, axis=1) + bx)
        log_a = -LRU_C * r * sp
        a = jnp.exp(log_a)
        t = jnp.tanh(log_a)
        b = jnp.sqrt(-2.0 * t / (1.0 - t)) * (gi * xc)
        a = a.reshape(LRU_TS // SUBLANES, SUBLANES, W)
        b = b.reshape(LRU_TS // SUBLANES, SUBLANES, W)
        for k in (1, 2, 4):
            keep = sub >= k
            b = jnp.where(keep, b + a * pltpu.roll(b, k, 1), b)
            a = jnp.where(keep, a * pltpu.roll(a, k, 1), a)
        p_ref[pl.ds(r0, LRU_TS), :] = a.reshape(LRU_TS, W)
        q_ref[pl.ds(r0, LRU_TS), :] = b.reshape(LRU_TS, W)
        return 0

    lax.fori_loop(0, SEQ // LRU_TS, gates, 0)

    def carry(gidx, h):
        r0 = pl.multiple_of(gidx * SUBLANES, SUBLANES)
        rows = p_ref[pl.ds(r0, SUBLANES), :] * h + q_ref[pl.ds(r0, SUBLANES), :]
        q_ref[pl.ds(r0, SUBLANES), :] = rows
        return jnp.broadcast_to(rows[SUBLANES - 1:SUBLANES, :], (SUBLANES, W))

    lax.fori_loop(0, SEQ // SUBLANES, carry, jnp.zeros((SUBLANES, W), F32), unroll=8)

    def out(c, _):
        r0 = pl.multiple_of(c * LRU_TS, LRU_TS)
        y = y_ref[pl.ds(r0, LRU_TS), :]
        o_ref[pl.ds(r0, LRU_TS), :] = (q_ref[pl.ds(r0, LRU_TS), :] * jax.nn.gelu(y)).astype(o_ref.dtype)
        return 0

    lax.fori_loop(0, SEQ // LRU_TS, out, 0)


def _rglru(proj, conv_w, conv_b, w_a, b_a, w_x, b_x, lam, l):
    W = LRU_WIDTH
    vec = _layer_vec(W, l)
    blk = pl.BlockSpec((None, LRU_BLOCKS, LRU_BLOCK, LRU_BLOCK), lambda b: (l, 0, 0, 0))
    return pl.pallas_call(
        _lru_kernel,
        grid=(BATCH,),
        in_specs=[
            pl.BlockSpec((SEQ, W), lambda b: (b, G_LX)),
            pl.BlockSpec((SEQ, W), lambda b: (b, G_LY)),
            pl.BlockSpec((None, CONV_WIDTH, W), lambda b: (l, 0, 0)),
            vec, blk, vec, blk, vec, vec,
        ],
        out_specs=pl.BlockSpec((SEQ, W), lambda b: (b, 0)),
        out_shape=jax.ShapeDtypeStruct((ROWS, W), BF16),
        scratch_shapes=[
            pltpu.VMEM((SEQ + LRU_PAD, W), F32),
            pltpu.VMEM((SEQ, W), F32),
            pltpu.VMEM((SEQ, W), F32),
        ],
        compiler_params=_cparams(("parallel",), 48),
        name="rglru",
    )(proj, proj, conv_w, conv_b, w_a, b_a, w_x, b_x, lam)


MERGE_TM, MERGE_TN = 1024, 512


def _merge_kernel(h_ref, o0_ref, o1_ref, o2_ref, o3_ref, wg_ref, wb_ref, out_ref):
    h = h_ref[...]
    acc = None
    for br, o_ref in enumerate((o0_ref, o1_ref, o2_ref, o3_ref)):
        t = jax.nn.sigmoid(_dot(h, wg_ref[br])) * _dot(o_ref[...], wb_ref[br])
        acc = t if acc is None else acc + t
    out_ref[...] = acc.astype(out_ref.dtype)


def _merge(h, branches, w_gate, w_branch, l):
    tm, tn = MERGE_TM, MERGE_TN
    obs = pl.BlockSpec((tm, BRANCH_WIDTH), lambda i, j: (i, 0))
    return pl.pallas_call(
        _merge_kernel,
        grid=(ROWS // tm, D_MODEL // tn),
        in_specs=[
            pl.BlockSpec((tm, D_MODEL), lambda i, j: (i, 0)),
            obs, obs, obs, obs,
            pl.BlockSpec((None, N_BRANCH, D_MODEL, tn), lambda i, j: (l, 0, 0, j)),
            pl.BlockSpec((None, N_BRANCH, BRANCH_WIDTH, tn), lambda i, j: (l, 0, 0, j)),
        ],
        out_specs=pl.BlockSpec((tm, tn), lambda i, j: (i, j)),
        out_shape=jax.ShapeDtypeStruct((ROWS, D_MODEL), BF16),
        compiler_params=_cparams(("parallel", "arbitrary"), 56),
        name="gated_merge",
    )(h, *branches, w_gate, w_branch)


OUT_TM, OUT_TN = 512, D_MODEL


def _outproj_kernel(m_ref, w_ref, x_ref, o_ref):
    o_ref[...] = x_ref[...] + _dot(m_ref[...], w_ref[...])


def _outproj(mixed, w_out, x, l):
    tm, tn = OUT_TM, OUT_TN
    return pl.pallas_call(
        _outproj_kernel,
        grid=(ROWS // tm, D_MODEL // tn),
        in_specs=[
            pl.BlockSpec((tm, D_MODEL), lambda i, j: (i, 0)),
            pl.BlockSpec((None, D_MODEL, tn), lambda i, j: (l, 0, j)),
            pl.BlockSpec((tm, tn), lambda i, j: (i, j)),
        ],
        out_specs=pl.BlockSpec((tm, tn), lambda i, j: (i, j)),
        out_shape=jax.ShapeDtypeStruct((ROWS, D_MODEL), F32),
        compiler_params=_cparams(("parallel", "arbitrary"), 44),
        name="out_proj",
    )(mixed, w_out, x)


FFN_TM, FFN_TF = 512, 512


def _ffn_kernel(x_ref, g_ref, wg_ref, wu_ref, wd_ref, gn_ref, *rest, final):
    if final:
        o_ref, h_ref = rest
        n_ref = o_ref
    else:
        o_ref, n_ref, h_ref = rest
    f = pl.program_id(1)

    @pl.when(f == 0)
    def _():
        _norm_into(x_ref, g_ref, h_ref, FFN_TM)
        o_ref[...] = x_ref[...]

    h = h_ref[...]
    a = _dot(h, wg_ref[...])
    act = (a * jax.nn.sigmoid(a) * _dot(h, wu_ref[...])).astype(BF16)
    o_ref[...] += _dot(act, wd_ref[...])

    @pl.when(f == pl.num_programs(1) - 1)
    def _():
        _norm_into(o_ref, gn_ref, n_ref, FFN_TM)


def _ffn(x, ln, wg, wu, wd, l, ln_next, l_next, final):
    tm, tf = FFN_TM, FFN_TF
    row = pl.BlockSpec((tm, D_MODEL), lambda i, f: (i, 0))
    out_f32 = jax.ShapeDtypeStruct((ROWS, D_MODEL), F32)
    return pl.pallas_call(
        functools.partial(_ffn_kernel, final=final),
        grid=(ROWS // tm, D_FF // tf),
        in_specs=[
            row,
            _layer_vec(D_MODEL, l),
            pl.BlockSpec((None, D_MODEL, tf), lambda i, f: (l, 0, f)),
            pl.BlockSpec((None, D_MODEL, tf), lambda i, f: (l, 0, f)),
            pl.BlockSpec((None, tf, D_MODEL), lambda i, f: (l, f, 0)),
            _layer_vec(D_MODEL, l_next),
        ],
        out_specs=row if final else [row, row],
        out_shape=out_f32 if final else [out_f32, jax.ShapeDtypeStruct((ROWS, D_MODEL), BF16)],
        scratch_shapes=[pltpu.VMEM((tm, D_MODEL), BF16)],
        compiler_params=_cparams(("parallel", "arbitrary"), 52),
        name="ffn_final" if final else "ffn",
    )(x, ln, wg, wu, wd, ln_next)


def _pack_w_in(w):
    lead = w.shape[:-1]
    mq0 = 3 * GROUP
    mq = w[..., mq0:mq0 + HEADS * (MLA_NOPE + MLA_ROPE)].reshape(*lead, HEADS, MLA_NOPE + MLA_ROPE)
    ckv0 = mq0 + HEADS * (MLA_NOPE + MLA_ROPE)
    kr0 = ckv0 + MLA_KV_RANK
    rest0 = kr0 + MLA_ROPE
    pe_pad = GROUP - HEADS * MLA_ROPE - MLA_ROPE
    parts = [
        w[..., :mq0],
        mq[..., :MLA_NOPE].reshape(*lead, HEADS * MLA_NOPE),
        w[..., ckv0:kr0],
        mq[..., MLA_NOPE:].reshape(*lead, HEADS * MLA_ROPE),
        w[..., kr0:rest0],
        jnp.zeros((*lead, pe_pad), w.dtype),
        w[..., rest0:],
    ]
    return jnp.concatenate(parts, axis=-1).astype(BF16)


def _pack_w_ukv(w):
    lead = w.shape[:-1]
    t = w.reshape(*lead, HEADS, 2 * HEAD_DIM)
    return jnp.concatenate([t[..., :HEAD_DIM].reshape(*lead, GROUP), t[..., HEAD_DIM:].reshape(*lead, GROUP)],
                           axis=-1).astype(BF16)


def kernel(x, positions, ln1, w_in, mla_kv_gain, mla_w_ukv, lru_conv_w, lru_conv_b, lru_w_a, lru_b_a,
           lru_w_x, lru_b_x, lru_lambda, w_branch, w_gate, w_out, ln2, ffn_w_gate, ffn_w_up,
           ffn_w_down, ln_final):
    ret_cos, ret_sin, mla_cos, mla_sin = _rope_tables(positions)
    ret_consts = _ret_consts()
    vec = lambda t: t.reshape(DEPTH, 1, -1)
    w_in_p = _pack_w_in(w_in)
    w_ukv_p = _pack_w_ukv(mla_w_ukv)
    w_a, w_x = lru_w_a.astype(BF16), lru_w_x.astype(BF16)
    w_gate_b, w_branch_b, w_out_b = w_gate.astype(BF16), w_branch.astype(BF16), w_out.astype(BF16)
    wf_gate, wf_up, wf_down = ffn_w_gate.astype(BF16), ffn_w_up.astype(BF16), ffn_w_down.astype(BF16)
    ln1, ln2, kv_gain = vec(ln1), vec(ln2), vec(mla_kv_gain)
    conv_b, b_a, b_x, lam = vec(lru_conv_b), vec(lru_b_a), vec(lru_b_x), vec(lru_lambda)

    xf = x.reshape(ROWS, D_MODEL)
    h = _norm(xf, ln1, 0)
    for l in range(DEPTH):
        proj = _proj(h, w_in_p, l)
        o_sb = _sb_attention(proj)
        kn, v, qpe, kpe = _mla_prep(proj, kv_gain, w_ukv_p, mla_cos, mla_sin, l)
        o_mla = _mla_attention(proj, kn, v, qpe, kpe)
        o_ret = _retention(proj, ret_cos, ret_sin, ret_consts)
        o_lru = _rglru(proj, lru_conv_w, conv_b, w_a, b_a, w_x, b_x, lam, l)
        mixed = _merge(h, (o_sb, o_mla, o_ret, o_lru), w_gate_b, w_branch_b, l)
        xf = _outproj(mixed, w_out_b, xf, l)
        if l < DEPTH - 1:
            xf, h = _ffn(xf, ln2, wf_gate, wf_up, wf_down, l, ln1, l + 1, final=False)
        else:
            xf = _ffn(xf, ln2, wf_gate, wf_up, wf_down, l, ln_final.reshape(1, 1, D_MODEL), 0, final=True)
    return xf.reshape(BATCH, SEQ, D_MODEL)
```

```python
import functools
import math

import numpy as np
import jax
import jax.numpy as jnp
from jax import lax
from jax.experimental import pallas as pl
from jax.experimental.pallas import tpu as pltpu

F32 = jnp.float32
BF16 = jnp.bfloat16

D_MODEL = 2048
BATCH = 8
SEQ = 2048
DEPTH = 4
ROWS = BATCH * SEQ
HEAD_DIM = 128
HEADS = 4
ROPE_THETA = 10000.0
EPS = 1e-6
N_BRANCH = 4
BRANCH_WIDTH = 512
MLA_NOPE = 128
MLA_ROPE = 64
MLA_KV_RANK = 512
RET_CHUNK = 128
LRU_WIDTH = 512
LRU_BLOCKS = 4
LRU_BLOCK = 128
CONV_WIDTH = 4
LRU_C = 8.0
D_FF = 5632

LANES = 128
SUBLANES = 8
GROUP = 512
G_SBQ, G_SBK, G_SBV, G_MQN, G_CKV, G_PE, G_RQ, G_RK, G_RV, G_RG, G_LX, G_LY = range(12)
PROJ_W = 12 * GROUP
HB = GROUP // LANES

VMEM_MB = 1024 * 1024
LOG2E = math.log2(math.e)


def _cparams(sem, vmem_mb):
    return pltpu.CompilerParams(dimension_semantics=sem, vmem_limit_bytes=vmem_mb * VMEM_MB)


def _dot(a, b):
    return jnp.dot(a, b, preferred_element_type=F32)


def _dot_nt(a, b):
    return lax.dot_general(a, b, (((1,), (1,)), ((), ())), preferred_element_type=F32)


def _dot_tn(a, b):
    return lax.dot_general(a, b, (((0,), (0,)), ((), ())), preferred_element_type=F32)


def _rmsnorm_rows(x, g):
    return x * lax.rsqrt(jnp.mean(x * x, axis=-1, keepdims=True) + EPS) * g


def _layer_vec(width, l):
    return pl.BlockSpec((None, 1, width), lambda *_: (l, 0, 0))


def _tables_kernel(pos_ref, fr_ref, sr_ref, fm_ref, sm_ref, rc_ref, rs_ref, mc_ref, ms_ref):
    pos = pos_ref[...]
    ang = pos * fr_ref[...]
    rc_ref[...] = jnp.cos(ang)
    rs_ref[...] = jnp.sin(ang) * sr_ref[...]
    ang = pos * fm_ref[...]
    mc_ref[...] = jnp.cos(ang)
    ms_ref[...] = jnp.sin(ang) * sm_ref[...]


def _rope_tables(positions):
    tm = 1024
    pos = jnp.broadcast_to(positions.reshape(ROWS, 1).astype(F32), (ROWS, LANES))

    def freq(d):
        return ROPE_THETA ** (-jnp.arange(0, d, 2, dtype=F32) / d)

    def sign(d):
        return jnp.concatenate([-jnp.ones((d // 2,), F32), jnp.ones((d // 2,), F32)])

    fr = jnp.tile(freq(HEAD_DIM), 2).reshape(1, LANES)
    sr = sign(HEAD_DIM).reshape(1, LANES)
    fm = jnp.tile(freq(MLA_ROPE), 4).reshape(1, LANES)
    sm = jnp.tile(sign(MLA_ROPE), 2).reshape(1, LANES)
    row = pl.BlockSpec((tm, LANES), lambda i: (i, 0))
    vec = pl.BlockSpec((1, LANES), lambda i: (0, 0))
    shp = jax.ShapeDtypeStruct((ROWS, LANES), F32)
    return pl.pallas_call(
        _tables_kernel,
        grid=(ROWS // tm,),
        in_specs=[row, vec, vec, vec, vec],
        out_specs=[row, row, row, row],
        out_shape=[shp, shp, shp, shp],
        compiler_params=_cparams(("parallel",), 32),
        name="rope_tables",
    )(pos, fr, sr, fm, sm)


NORM_TM = 512
NORM_CHUNK = 256


def _norm_into(x_ref, g_ref, h_ref, rows):
    g = g_ref[...]

    def body(c, _):
        r0 = pl.multiple_of(c * NORM_CHUNK, NORM_CHUNK)
        x = x_ref[pl.ds(r0, NORM_CHUNK), :]
        h_ref[pl.ds(r0, NORM_CHUNK), :] = _rmsnorm_rows(x, g).astype(h_ref.dtype)
        return 0

    lax.fori_loop(0, rows // NORM_CHUNK, body, 0)


def _norm_kernel(x_ref, g_ref, h_ref):
    _norm_into(x_ref, g_ref, h_ref, NORM_TM)


def _norm(x, ln, l):
    tm = NORM_TM
    return pl.pallas_call(
        _norm_kernel,
        grid=(ROWS // tm,),
        in_specs=[pl.BlockSpec((tm, D_MODEL), lambda i: (i, 0)), _layer_vec(D_MODEL, l)],
        out_specs=pl.BlockSpec((tm, D_MODEL), lambda i: (i, 0)),
        out_shape=jax.ShapeDtypeStruct((ROWS, D_MODEL), BF16),
        compiler_params=_cparams(("parallel",), 32),
        name="input_norm",
    )(x, ln)


PROJ_TM, PROJ_TN = 2048, 1024


def _proj_kernel(h_ref, w_ref, o_ref):
    o_ref[...] = _dot(h_ref[...], w_ref[...])


def _proj(h, w, l):
    tm, tn = PROJ_TM, PROJ_TN
    return pl.pallas_call(
        _proj_kernel,
        grid=(ROWS // tm, PROJ_W // tn),
        in_specs=[
            pl.BlockSpec((tm, D_MODEL), lambda i, j: (i, 0)),
            pl.BlockSpec((None, D_MODEL, tn), lambda i, j: (l, 0, j)),
        ],
        out_specs=pl.BlockSpec((tm, tn), lambda i, j: (i, j)),
        out_shape=jax.ShapeDtypeStruct((ROWS, PROJ_W), F32),
        compiler_params=_cparams(("parallel", "arbitrary"), 52),
        name="in_proj",
    )(h, w)


ATT_HEADS = 4
SB_TK = 128
SB_GROUP = 4
SB_TQ = SB_GROUP * SB_TK


def _sb_weights(z, tri, carry, masks):
    nblk = len(masks)
    ls_l, r_l = [], []
    for j in range(nblk - 1, -1, -1):
        zj = z[:, j * SB_TK:(j + 1) * SB_TK]
        m = jnp.maximum(zj, 0.0) + jnp.log2(1.0 + jnp.exp2(-jnp.abs(zj)))
        ls_l.append(zj - m)
        if masks[j] is not None:
            m = jnp.where(masks[j], m, 0.0)
        r_l.append(_dot(m.astype(BF16), tri))
    ws = []
    for j, ls, r in zip(range(nblk - 1, -1, -1), ls_l, r_l):
        w = jnp.exp2(ls + r[:, :SB_TK] + carry)
        if masks[j] is not None:
            w = jnp.where(masks[j], w, 0.0)
        ws.append(w.astype(BF16))
        carry = carry + r[:, SB_TK:]
    return carry, jnp.concatenate(ws[::-1], axis=1)


def _sb_kernel(q_ref, k_ref, v_ref, tri_ref, o_ref, kb_ref, vb_ref):
    i = pl.program_id(2)

    @pl.when(i == 0)
    def _():
        kb_ref[...] = k_ref[...].astype(BF16)
        vb_ref[...] = v_ref[...].astype(BF16)

    tri = tri_ref[...]
    heads = [slice(hh * HEAD_DIM, (hh + 1) * HEAD_DIM) for hh in range(ATT_HEADS)]
    qs = [(q_ref[:, hd] * (HEAD_DIM ** -0.5 * LOG2E)).astype(BF16) for hd in heads]
    half, hblk = SB_TQ // 2, SB_GROUP // 2
    row = lax.broadcasted_iota(jnp.int32, (half, SB_TK), 0)
    col = lax.broadcasted_iota(jnp.int32, (half, SB_TK), 1)
    causal = [(col + j * SB_TK) < row for j in range(hblk)]
    k_diag = pl.multiple_of(i * SB_TQ, SB_TQ)

    def part(q, hd, k0, nkeys, carry, masks):
        carry, w = _sb_weights(_dot_nt(q, kb_ref[pl.ds(k0, nkeys), hd]), tri, carry, masks)
        return carry, _dot(w, vb_ref[pl.ds(k0, nkeys), hd])

    def diagonal(q, hd):
        zero = jnp.zeros((half, SB_TK), F32)
        top = part(q[:half], hd, k_diag, half, zero, causal)
        bot = part(q[half:], hd, k_diag, SB_TQ, zero, [None] * hblk + causal)
        return tuple(jnp.concatenate([t, b], axis=0) for t, b in zip(top, bot))

    def group(g, state):
        k0 = pl.multiple_of(g * SB_TQ, SB_TQ)
        out = []
        for q, hd, (carry, acc) in zip(qs, heads, state):
            carry, pv = part(q, hd, k0, SB_TQ, carry, [None] * SB_GROUP)
            out.append((carry, acc + pv))
        return tuple(out)

    state = tuple(diagonal(q, hd) for q, hd in zip(qs, heads))
    state = lax.fori_loop(0, i, lambda t, s: group(i - 1 - t, s), state)
    for hd, (_, acc) in zip(heads, state):
        o_ref[:, hd] = acc.astype(o_ref.dtype)


def _sb_tri():
    j = np.arange(SB_TK)
    t = (j[:, None] > j[None, :]).astype(np.float32)
    return jnp.asarray(-np.concatenate([t, np.ones((SB_TK, SB_TK), np.float32)], axis=1), dtype=BF16)


def _sb_attention(proj):
    nq = SEQ // SB_TQ
    hw = ATT_HEADS * HEAD_DIM
    hs = HEADS // ATT_HEADS
    return pl.pallas_call(
        _sb_kernel,
        grid=(BATCH, hs, nq),
        in_specs=[
            pl.BlockSpec((SB_TQ, hw), lambda b, h, i: (b * nq + i, G_SBQ * hs + h)),
            pl.BlockSpec((SEQ, hw), lambda b, h, i: (b, G_SBK * hs + h)),
            pl.BlockSpec((SEQ, hw), lambda b, h, i: (b, G_SBV * hs + h)),
            pl.BlockSpec((SB_TK, 2 * SB_TK), lambda b, h, i: (0, 0)),
        ],
        out_specs=pl.BlockSpec((SB_TQ, hw), lambda b, h, i: (b * nq + i, h)),
        out_shape=jax.ShapeDtypeStruct((ROWS, BRANCH_WIDTH), BF16),
        scratch_shapes=[pltpu.VMEM((SEQ, hw), BF16), pltpu.VMEM((SEQ, hw), BF16)],
        compiler_params=_cparams(("parallel", "parallel", "arbitrary"), 32),
        name="sb_attention",
    )(proj, proj, proj, _sb_tri())


MLA_TM = 1024
MLA_SCALE = (MLA_NOPE + MLA_ROPE) ** -0.5 * LOG2E


def _rope64(x, cos, sin_signed):
    lane = lax.broadcasted_iota(jnp.int32, x.shape, 1)
    first = (lane % MLA_ROPE) < (MLA_ROPE // 2)
    rot = jnp.where(first, pltpu.roll(x, LANES - MLA_ROPE // 2, 1), pltpu.roll(x, MLA_ROPE // 2, 1))
    return x * cos + rot * sin_signed


def _mla_prep_kernel(c_ref, pe_ref, gain_ref, w_ref, cos_ref, sin_ref,
                     kn_ref, v_ref, qpe_ref, kpe_ref):
    cn = _rmsnorm_rows(c_ref[...], gain_ref[...]).astype(BF16)
    kv = _dot(cn, w_ref[...])
    kn_ref[...] = kv[:, :GROUP].astype(BF16)
    v_ref[...] = kv[:, GROUP:].astype(BF16)
    cos = cos_ref[...]
    sin = sin_ref[...]
    lane = lax.broadcasted_iota(jnp.int32, (MLA_TM, LANES), 1)
    low = lane < MLA_ROPE
    for c in range(2):
        r = _rope64(pe_ref[:, c * LANES:(c + 1) * LANES], cos, sin) * MLA_SCALE
        qpe_ref[:, (2 * c) * LANES:(2 * c + 1) * LANES] = jnp.where(low, r, 0.0).astype(BF16)
        qpe_ref[:, (2 * c + 1) * LANES:(2 * c + 2) * LANES] = jnp.where(
            low, pltpu.roll(r, MLA_ROPE, 1), 0.0).astype(BF16)
    r = _rope64(pe_ref[:, 2 * LANES:3 * LANES], cos, sin)
    kpe_ref[...] = jnp.where(low, r, 0.0).astype(BF16)


def _mla_prep(proj, gain, w_ukv, cos, sin, l):
    tm = MLA_TM
    row512 = pl.BlockSpec((tm, GROUP), lambda i: (i, 0))
    row128 = pl.BlockSpec((tm, LANES), lambda i: (i, 0))
    return pl.pallas_call(
        _mla_prep_kernel,
        grid=(ROWS // tm,),
        in_specs=[
            pl.BlockSpec((tm, GROUP), lambda i: (i, G_CKV)),
            pl.BlockSpec((tm, GROUP), lambda i: (i, G_PE)),
            _layer_vec(MLA_KV_RANK, l),
            pl.BlockSpec((None, MLA_KV_RANK, 2 * GROUP), lambda i: (l, 0, 0)),
            row128, row128,
        ],
        out_specs=[row512, row512, row512, row128],
        out_shape=[
            jax.ShapeDtypeStruct((ROWS, GROUP), BF16),
            jax.ShapeDtypeStruct((ROWS, GROUP), BF16),
            jax.ShapeDtypeStruct((ROWS, GROUP), BF16),
            jax.ShapeDtypeStruct((ROWS, LANES), BF16),
        ],
        compiler_params=_cparams(("parallel",), 40),
        name="mla_prep",
    )(proj, proj, gain, w_ukv, cos, sin)


MLA_T = 512


def _mla_step(q, k, v, m, l, acc, mask):
    s = _dot_nt(q, k)
    if mask is not None:
        s = jnp.where(mask, s, -jnp.inf)
    m_new = jnp.maximum(m, jnp.max(s, axis=-1, keepdims=True))
    alpha = jnp.exp2(m - m_new)
    p = jnp.exp2(s - m_new)
    l = alpha * l + jnp.sum(p, axis=-1, keepdims=True)
    acc = alpha * acc + _dot(p.astype(BF16), v)
    return m_new, l, acc


def _mla_kernel(qn_ref, qpe_ref, kn_ref, kpe_ref, v_ref, o_ref, kc_ref):
    i = pl.program_id(2)
    heads = [slice(hh * HEAD_DIM, (hh + 1) * HEAD_DIM) for hh in range(ATT_HEADS)]

    @pl.when(i == 0)
    def _():
        for hh, hd in enumerate(heads):
            kc_ref[hh, :, :LANES] = kn_ref[:, hd]
            kc_ref[hh, :, LANES:] = kpe_ref[...]

    qs = [jnp.concatenate([(qn_ref[:, hd] * MLA_SCALE).astype(BF16), qpe_ref[:, hd]], axis=1)
          for hd in heads]
    row = lax.broadcasted_iota(jnp.int32, (MLA_T, MLA_T), 0)
    col = lax.broadcasted_iota(jnp.int32, (MLA_T, MLA_T), 1)

    def block(g, state, mask):
        k0 = pl.multiple_of(g * MLA_T, MLA_T)
        return tuple(
            _mla_step(q, kc_ref[hh, pl.ds(k0, MLA_T), :], v_ref[pl.ds(k0, MLA_T), hd], m, l, acc, mask)
            for hh, (q, hd, (m, l, acc)) in enumerate(zip(qs, heads, state)))

    init = (jnp.full((MLA_T, 1), -jnp.inf, F32), jnp.zeros((MLA_T, 1), F32), jnp.zeros((MLA_T, HEAD_DIM), F32))
    state = block(i, (init,) * ATT_HEADS, col <= row)
    state = lax.fori_loop(0, i, lambda t, s: block(t, s, None), state)
    for hd, (_, l, acc) in zip(heads, state):
        o_ref[:, hd] = (acc / l).astype(o_ref.dtype)


def _mla_attention(proj, kn, v, qpe, kpe):
    nq = SEQ // MLA_T
    hw = ATT_HEADS * HEAD_DIM
    hs = HEADS // ATT_HEADS
    return pl.pallas_call(
        _mla_kernel,
        grid=(BATCH, hs, nq),
        in_specs=[
            pl.BlockSpec((MLA_T, hw), lambda b, h, i: (b * nq + i, G_MQN * hs + h)),
            pl.BlockSpec((MLA_T, hw), lambda b, h, i: (b * nq + i, h)),
            pl.BlockSpec((SEQ, hw), lambda b, h, i: (b, h)),
            pl.BlockSpec((SEQ, LANES), lambda b, h, i: (b, 0)),
            pl.BlockSpec((SEQ, hw), lambda b, h, i: (b, h)),
        ],
        out_specs=pl.BlockSpec((MLA_T, hw), lambda b, h, i: (b * nq + i, h)),
        out_shape=jax.ShapeDtypeStruct((ROWS, BRANCH_WIDTH), BF16),
        scratch_shapes=[pltpu.VMEM((ATT_HEADS, SEQ, 2 * LANES), BF16)],
        compiler_params=_cparams(("parallel", "parallel", "arbitrary"), 32),
        name="mla_attention",
    )(proj, qpe, kn, kpe, v)


def _rope128(x, cos, sin_signed):
    return x * cos + pltpu.roll(x, HEAD_DIM // 2, 1) * sin_signed


def _ret_kernel(q_ref, k_ref, v_ref, g_ref, cos_ref, sin_ref, dec_ref, qd_ref, kd_ref, cd_ref, o_ref):
    C = RET_CHUNK
    dec = dec_ref[...]
    qd = qd_ref[...]
    kd = kd_ref[...]
    cd = cd_ref[...]
    state = jnp.zeros((HEAD_DIM, HEAD_DIM), F32)
    for n in range(SEQ // C):
        sl = pl.ds(n * C, C)
        cos = cos_ref[sl, :]
        sin = sin_ref[sl, :]
        q = _rope128(q_ref[sl, :], cos, sin)
        k = _rope128(k_ref[sl, :], cos, sin) * (HEAD_DIM ** -0.5)
        v = v_ref[sl, :].astype(BF16)
        scores = _dot_nt(q.astype(BF16), k.astype(BF16)) * dec
        inner = _dot(scores.astype(BF16), v)
        cross = _dot((q * qd).astype(BF16), state.astype(BF16))
        state = state * cd + _dot_tn((k * kd).astype(BF16), v)
        o = inner + cross
        o = o * lax.rsqrt(jnp.mean(o * o, axis=-1, keepdims=True) + EPS)
        g = g_ref[sl, :]
        o_ref[sl, :] = (o * (g * jax.nn.sigmoid(g))).astype(o_ref.dtype)


def _ret_consts():
    C = RET_CHUNK
    log_gamma = jnp.log1p(-(2.0 ** (-5.0 - jnp.arange(HEADS, dtype=F32))))
    idx = jnp.arange(C, dtype=F32)
    diff = idx[:, None] - idx[None, :]
    decay = jnp.where(diff >= 0, jnp.exp(log_gamma[:, None, None] * jnp.maximum(diff, 0.0)), 0.0)
    k_decay = jnp.exp(log_gamma[:, None] * (C - 1.0 - idx)[None, :])
    q_decay = jnp.exp(log_gamma[:, None] * (idx + 1.0)[None, :])
    chunk_decay = jnp.exp(log_gamma * C)
    bc = lambda t: jnp.broadcast_to(t[:, :, None], (HEADS, C, HEAD_DIM))
    cdm = jnp.broadcast_to(chunk_decay[:, None, None], (HEADS, HEAD_DIM, HEAD_DIM))
    return decay, bc(q_decay), bc(k_decay), cdm


def _retention(proj, cos, sin, consts):
    seq = lambda g: pl.BlockSpec((SEQ, HEAD_DIM), lambda b, h: (b, g * HB + h))
    tab = pl.BlockSpec((SEQ, LANES), lambda b, h: (b, 0))
    cst = pl.BlockSpec((None, HEAD_DIM, HEAD_DIM), lambda b, h: (h, 0, 0))
    return pl.pallas_call(
        _ret_kernel,
        grid=(BATCH, HEADS),
        in_specs=[seq(G_RQ), seq(G_RK), seq(G_RV), seq(G_RG), tab, tab, cst, cst, cst, cst],
        out_specs=pl.BlockSpec((SEQ, HEAD_DIM), lambda b, h: (b, h)),
        out_shape=jax.ShapeDtypeStruct((ROWS, BRANCH_WIDTH), BF16),
        compiler_params=_cparams(("parallel", "parallel"), 32),
        name="retention",
    )(proj, proj, proj, proj, cos, sin, *consts)


LRU_TS = 256
LRU_PAD = 8


def _lru_kernel(x_ref, y_ref, cw_ref, cb_ref, wa_ref, ba_ref, wx_ref, bx_ref, lam_ref, o_ref,
                xp_ref, p_ref, q_ref):
    W = LRU_WIDTH
    xp_ref[0:LRU_PAD, :] = jnp.zeros((LRU_PAD, W), F32)
    cw = cw_ref[...]
    cb = cb_ref[...]
    ba = ba_ref[...]
    bx = bx_ref[...]
    nlam = -lam_ref[...]
    sp = jnp.maximum(nlam, 0.0) + jnp.log1p(jnp.exp(-jnp.abs(nlam)))
    sub = lax.broadcasted_iota(jnp.int32, (LRU_TS // SUBLANES, SUBLANES, W), 1)

    def copy(c, _):
        r0 = pl.multiple_of(c * LRU_TS, LRU_TS)
        xp_ref[pl.ds(LRU_PAD + r0, LRU_TS), :] = x_ref[pl.ds(r0, LRU_TS), :]
        return 0

    lax.fori_loop(0, SEQ // LRU_TS, copy, 0)

    def gates(c, _):
        r0 = pl.multiple_of(c * LRU_TS, LRU_TS)
        win = xp_ref[pl.ds(r0, LRU_TS + LRU_PAD), :].reshape(LRU_TS // SUBLANES + 1, SUBLANES, W)
        xc = cb
        for tap in range(CONV_WIDTH):
            s = CONV_WIDTH - 1 - tap
            rolled = pltpu.roll(win, s, 1) if s else win
            shifted = jnp.where(sub >= s, rolled[1:], rolled[:-1]) if s else win[1:]
            xc = xc + shifted * cw[tap:tap + 1, :]
        xc = xc.reshape(LRU_TS, W)
        ra, ri = [], []
        for blk in range(LRU_BLOCKS):
            xb = xc[:, blk * LRU_BLOCK:(blk + 1) * LRU_BLOCK].astype(BF16)
            ra.append(_dot(xb, wa_ref[blk]))
            ri.append(_dot(xb, wx_ref[blk]))
        r = jax.nn.sigmoid(jnp.concatenate(ra, axis=1) + ba)
        gi = jax.nn.sigmoid(jnp.concatenate(ri, axis=1) + bx)
        log_a = -LRU_C * r * sp
        a = jnp.exp(log_a)
        t = jnp.tanh(log_a)
        b = jnp.sqrt(-2.0 * t / (1.0 - t)) * (gi * xc)
        a = a.reshape(LRU_TS // SUBLANES, SUBLANES, W)
        b = b.reshape(LRU_TS // SUBLANES, SUBLANES, W)
        for k in (1, 2, 4):
            keep = sub >= k
            b = jnp.where(keep, b + a * pltpu.roll(b, k, 1), b)
            a = jnp.where(keep, a * pltpu.roll(a, k, 1), a)
        p_ref[pl.ds(r0, LRU_TS), :] = a.reshape(LRU_TS, W)
        q_ref[pl.ds(r0, LRU_TS), :] = b.reshape(LRU_TS, W)
        return 0

    lax.fori_loop(0, SEQ // LRU_TS, gates, 0)

    def carry(gidx, h):
        r0 = pl.multiple_of(gidx * SUBLANES, SUBLANES)
        rows = p_ref[pl.ds(r0, SUBLANES), :] * h + q_ref[pl.ds(r0, SUBLANES), :]
        q_ref[pl.ds(r0, SUBLANES), :] = rows
        return jnp.broadcast_to(rows[SUBLANES - 1:SUBLANES, :], (SUBLANES, W))

    lax.fori_loop(0, SEQ // SUBLANES, carry, jnp.zeros((SUBLANES, W), F32), unroll=8)

    def out(c, _):
        r0 = pl.multiple_of(c * LRU_TS, LRU_TS)
        y = y_ref[pl.ds(r0, LRU_TS), :]
        o_ref[pl.ds(r0, LRU_TS), :] = (q_ref[pl.ds(r0, LRU_TS), :] * jax.nn.gelu(y)).astype(o_ref.dtype)
        return 0

    lax.fori_loop(0, SEQ // LRU_TS, out, 0)


def _rglru(proj, conv_w, conv_b, w_a, b_a, w_x, b_x, lam, l):
    W = LRU_WIDTH
    vec = _layer_vec(W, l)
    blk = pl.BlockSpec((None, LRU_BLOCKS, LRU_BLOCK, LRU_BLOCK), lambda b: (l, 0, 0, 0))
    return pl.pallas_call(
        _lru_kernel,
        grid=(BATCH,),
        in_specs=[
            pl.BlockSpec((SEQ, W), lambda b: (b, G_LX)),
            pl.BlockSpec((SEQ, W), lambda b: (b, G_LY)),
            pl.BlockSpec((None, CONV_WIDTH, W), lambda b: (l, 0, 0)),
            vec, blk, vec, blk, vec, vec,
        ],
        out_specs=pl.BlockSpec((SEQ, W), lambda b: (b, 0)),
        out_shape=jax.ShapeDtypeStruct((ROWS, W), BF16),
        scratch_shapes=[
            pltpu.VMEM((SEQ + LRU_PAD, W), F32),
            pltpu.VMEM((SEQ, W), F32),
            pltpu.VMEM((SEQ, W), F32),
        ],
        compiler_params=_cparams(("parallel",), 48),
        name="rglru",
    )(proj, proj, conv_w, conv_b, w_a, b_a, w_x, b_x, lam)


MERGE_TM, MERGE_TN = 1024, 512


def _merge_kernel(h_ref, o0_ref, o1_ref, o2_ref, o3_ref, wg_ref, wb_ref, out_ref):
    h = h_ref[...]
    acc = None
    for br, o_ref in enumerate((o0_ref, o1_ref, o2_ref, o3_ref)):
        t = jax.nn.sigmoid(_dot(h, wg_ref[br])) * _dot(o_ref[...], wb_ref[br])
        acc = t if acc is None else acc + t
    out_ref[...] = acc.astype(out_ref.dtype)


def _merge(h, branches, w_gate, w_branch, l):
    tm, tn = MERGE_TM, MERGE_TN
    obs = pl.BlockSpec((tm, BRANCH_WIDTH), lambda i, j: (i, 0))
    return pl.pallas_call(
        _merge_kernel,
        grid=(ROWS // tm, D_MODEL // tn),
        in_specs=[
            pl.BlockSpec((tm, D_MODEL), lambda i, j: (i, 0)),
            obs, obs, obs, obs,
            pl.BlockSpec((None, N_BRANCH, D_MODEL, tn), lambda i, j: (l, 0, 0, j)),
            pl.BlockSpec((None, N_BRANCH, BRANCH_WIDTH, tn), lambda i, j: (l, 0, 0, j)),
        ],
        out_specs=pl.BlockSpec((tm, tn), lambda i, j: (i, j)),
        out_shape=jax.ShapeDtypeStruct((ROWS, D_MODEL), BF16),
        compiler_params=_cparams(("parallel", "arbitrary"), 56),
        name="gated_merge",
    )(h, *branches, w_gate, w_branch)


OUT_TM, OUT_TN = 512, D_MODEL


def _outproj_kernel(m_ref, w_ref, x_ref, o_ref):
    o_ref[...] = x_ref[...] + _dot(m_ref[...], w_ref[...])


def _outproj(mixed, w_out, x, l):
    tm, tn = OUT_TM, OUT_TN
    return pl.pallas_call(
        _outproj_kernel,
        grid=(ROWS // tm, D_MODEL // tn),
        in_specs=[
            pl.BlockSpec((tm, D_MODEL), lambda i, j: (i, 0)),
            pl.BlockSpec((None, D_MODEL, tn), lambda i, j: (l, 0, j)),
            pl.BlockSpec((tm, tn), lambda i, j: (i, j)),
        ],
        out_specs=pl.BlockSpec((tm, tn), lambda i, j: (i, j)),
        out_shape=jax.ShapeDtypeStruct((ROWS, D_MODEL), F32),
        compiler_params=_cparams(("parallel", "arbitrary"), 44),
        name="out_proj",
    )(mixed, w_out, x)


FFN_TM, FFN_TF = 512, 1024
FFN_STEPS = pl.cdiv(D_FF, FFN_TF)
FFN_TAIL = D_FF - (FFN_STEPS - 1) * FFN_TF


def _ffn_kernel(x_ref, g_ref, wg_ref, wu_ref, wd_ref, gn_ref, *rest, final):
    if final:
        o_ref, h_ref = rest
        n_ref = o_ref
    else:
        o_ref, n_ref, h_ref = rest
    f = pl.program_id(1)

    @pl.when(f == 0)
    def _():
        _norm_into(x_ref, g_ref, h_ref, FFN_TM)
        o_ref[...] = x_ref[...]

    def hidden_tile(width):
        h = h_ref[...]
        a = _dot(h, wg_ref[:, :width])
        act = (a * jax.nn.sigmoid(a) * _dot(h, wu_ref[:, :width])).astype(BF16)
        o_ref[...] += _dot(act, wd_ref[:width, :])

    pl.when(f < FFN_STEPS - 1)(lambda: hidden_tile(FFN_TF))

    @pl.when(f == FFN_STEPS - 1)
    def _():
        hidden_tile(FFN_TAIL)
        _norm_into(o_ref, gn_ref, n_ref, FFN_TM)


def _ffn(x, ln, wg, wu, wd, l, ln_next, l_next, final):
    tm, tf = FFN_TM, FFN_TF
    row = pl.BlockSpec((tm, D_MODEL), lambda i, f: (i, 0))
    out_f32 = jax.ShapeDtypeStruct((ROWS, D_MODEL), F32)
    return pl.pallas_call(
        functools.partial(_ffn_kernel, final=final),
        grid=(ROWS // tm, FFN_STEPS),
        in_specs=[
            row,
            _layer_vec(D_MODEL, l),
            pl.BlockSpec((None, D_MODEL, tf), lambda i, f: (l, 0, f)),
            pl.BlockSpec((None, D_MODEL, tf), lambda i, f: (l, 0, f)),
            pl.BlockSpec((None, tf, D_MODEL), lambda i, f: (l, f, 0)),
            _layer_vec(D_MODEL, l_next),
        ],
        out_specs=row if final else [row, row],
        out_shape=out_f32 if final else [out_f32, jax.ShapeDtypeStruct((ROWS, D_MODEL), BF16)],
        scratch_shapes=[pltpu.VMEM((tm, D_MODEL), BF16)],
        compiler_params=_cparams(("parallel", "arbitrary"), 60),
        name="ffn_final" if final else "ffn",
    )(x, ln, wg, wu, wd, ln_next)


def _pack_w_in(w):
    w = w.astype(BF16)
    lead = w.shape[:-1]
    mq0 = 3 * GROUP
    mq = w[..., mq0:mq0 + HEADS * (MLA_NOPE + MLA_ROPE)].reshape(*lead, HEADS, MLA_NOPE + MLA_ROPE)
    ckv0 = mq0 + HEADS * (MLA_NOPE + MLA_ROPE)
    kr0 = ckv0 + MLA_KV_RANK
    rest0 = kr0 + MLA_ROPE
    pe_pad = GROUP - HEADS * MLA_ROPE - MLA_ROPE
    parts = [
        w[..., :mq0],
        mq[..., :MLA_NOPE].reshape(*lead, HEADS * MLA_NOPE),
        w[..., ckv0:kr0],
        mq[..., MLA_NOPE:].reshape(*lead, HEADS * MLA_ROPE),
        w[..., kr0:rest0],
        jnp.zeros((*lead, pe_pad), w.dtype),
        w[..., rest0:],
    ]
    return jnp.concatenate(parts, axis=-1)


def _pack_w_ukv(w):
    w = w.astype(BF16)
    lead = w.shape[:-1]
    t = w.reshape(*lead, HEADS, 2 * HEAD_DIM)
    return jnp.concatenate([t[..., :HEAD_DIM].reshape(*lead, GROUP), t[..., HEAD_DIM:].reshape(*lead, GROUP)],
                           axis=-1)


def kernel(x, positions, ln1, w_in, mla_kv_gain, mla_w_ukv, lru_conv_w, lru_conv_b, lru_w_a, lru_b_a,
           lru_w_x, lru_b_x, lru_lambda, w_branch, w_gate, w_out, ln2, ffn_w_gate, ffn_w_up,
           ffn_w_down, ln_final):
    ret_cos, ret_sin, mla_cos, mla_sin = _rope_tables(positions)
    ret_consts = _ret_consts()
    vec = lambda t: t.reshape(DEPTH, 1, -1)
    w_in_p = _pack_w_in(w_in)
    w_ukv_p = _pack_w_ukv(mla_w_ukv)
    w_a, w_x = lru_w_a.astype(BF16), lru_w_x.astype(BF16)
    w_gate_b, w_branch_b, w_out_b = w_gate.astype(BF16), w_branch.astype(BF16), w_out.astype(BF16)
    wf_gate, wf_up, wf_down = ffn_w_gate.astype(BF16), ffn_w_up.astype(BF16), ffn_w_down.astype(BF16)
    ln1, ln2, kv_gain = vec(ln1), vec(ln2), vec(mla_kv_gain)
    conv_b, b_a, b_x, lam = vec(lru_conv_b), vec(lru_b_a), vec(lru_b_x), vec(lru_lambda)

    xf = x.reshape(ROWS, D_MODEL)
    h = _norm(xf, ln1, 0)
    for l in range(DEPTH):
        proj = _proj(h, w_in_p, l)
        o_sb = _sb_attention(proj)
        kn, v, qpe, kpe = _mla_prep(proj, kv_gain, w_ukv_p, mla_cos, mla_sin, l)
        o_mla = _mla_attention(proj, kn, v, qpe, kpe)
        o_ret = _retention(proj, ret_cos, ret_sin, ret_consts)
        o_lru = _rglru(proj, lru_conv_w, conv_b, w_a, b_a, w_x, b_x, lam, l)
        mixed = _merge(h, (o_sb, o_mla, o_ret, o_lru), w_gate_b, w_branch_b, l)
        xf = _outproj(mixed, w_out_b, xf, l)
        if l < DEPTH - 1:
            xf, h = _ffn(xf, ln2, wf_gate, wf_up, wf_down, l, ln1, l + 1, final=False)
        else:
            xf = _ffn(xf, ln2, wf_gate, wf_up, wf_down, l, ln_final.reshape(1, 1, D_MODEL), 0, final=True)
    return xf.reshape(BATCH, SEQ, D_MODEL)
```

```python
import functools
import math

import numpy as np
import jax
import jax.numpy as jnp
from jax import lax
from jax.experimental import pallas as pl
from jax.experimental.pallas import tpu as pltpu

F32 = jnp.float32
BF16 = jnp.bfloat16

D_MODEL = 2048
BATCH = 8
SEQ = 2048
DEPTH = 4
ROWS = BATCH * SEQ
HEAD_DIM = 128
HEADS = 4
ROPE_THETA = 10000.0
EPS = 1e-6
N_BRANCH = 4
BRANCH_WIDTH = 512
MLA_NOPE = 128
MLA_ROPE = 64
MLA_KV_RANK = 512
RET_CHUNK = 128
LRU_WIDTH = 512
LRU_BLOCKS = 4
LRU_BLOCK = 128
CONV_WIDTH = 4
LRU_C = 8.0
D_FF = 5632

LANES = 128
SUBLANES = 8
GROUP = 512
G_SBQ, G_SBK, G_SBV, G_MQN, G_CKV, G_PE, G_RQ, G_RK, G_RV, G_RG, G_LX, G_LY = range(12)
PROJ_W = 12 * GROUP
HB = GROUP // LANES

VMEM_MB = 1024 * 1024
LOG2E = math.log2(math.e)


def _cparams(sem, vmem_mb):
    return pltpu.CompilerParams(dimension_semantics=sem, vmem_limit_bytes=vmem_mb * VMEM_MB)


def _dot(a, b):
    return jnp.dot(a, b, preferred_element_type=F32)


def _dot_nt(a, b):
    return lax.dot_general(a, b, (((1,), (1,)), ((), ())), preferred_element_type=F32)


def _dot_tn(a, b):
    return lax.dot_general(a, b, (((0,), (0,)), ((), ())), preferred_element_type=F32)


def _rmsnorm_rows(x, g):
    return x * lax.rsqrt(jnp.mean(x * x, axis=-1, keepdims=True) + EPS) * g


def _layer_vec(width, l):
    return pl.BlockSpec((None, 1, width), lambda *_: (l, 0, 0))


def _tables_kernel(pos_ref, fr_ref, sr_ref, fm_ref, sm_ref, rc_ref, rs_ref, mc_ref, ms_ref):
    pos = pos_ref[...]
    ang = pos * fr_ref[...]
    rc_ref[...] = jnp.cos(ang)
    rs_ref[...] = jnp.sin(ang) * sr_ref[...]
    ang = pos * fm_ref[...]
    mc_ref[...] = jnp.cos(ang)
    ms_ref[...] = jnp.sin(ang) * sm_ref[...]


def _rope_tables(positions):
    tm = 1024
    pos = jnp.broadcast_to(positions.reshape(ROWS, 1).astype(F32), (ROWS, LANES))

    def freq(d):
        return ROPE_THETA ** (-jnp.arange(0, d, 2, dtype=F32) / d)

    def sign(d):
        return jnp.concatenate([-jnp.ones((d // 2,), F32), jnp.ones((d // 2,), F32)])

    fr = jnp.tile(freq(HEAD_DIM), 2).reshape(1, LANES)
    sr = sign(HEAD_DIM).reshape(1, LANES)
    fm = jnp.tile(freq(MLA_ROPE), 4).reshape(1, LANES)
    sm = jnp.tile(sign(MLA_ROPE), 2).reshape(1, LANES)
    row = pl.BlockSpec((tm, LANES), lambda i: (i, 0))
    vec = pl.BlockSpec((1, LANES), lambda i: (0, 0))
    shp = jax.ShapeDtypeStruct((ROWS, LANES), F32)
    return pl.pallas_call(
        _tables_kernel,
        grid=(ROWS // tm,),
        in_specs=[row, vec, vec, vec, vec],
        out_specs=[row, row, row, row],
        out_shape=[shp, shp, shp, shp],
        compiler_params=_cparams(("parallel",), 32),
        name="rope_tables",
    )(pos, fr, sr, fm, sm)


NORM_TM = 512
NORM_CHUNK = 256


def _norm_into(x_ref, g_ref, h_ref, rows):
    g = g_ref[...]

    def body(c, _):
        r0 = pl.multiple_of(c * NORM_CHUNK, NORM_CHUNK)
        x = x_ref[pl.ds(r0, NORM_CHUNK), :]
        h_ref[pl.ds(r0, NORM_CHUNK), :] = _rmsnorm_rows(x, g).astype(h_ref.dtype)
        return 0

    lax.fori_loop(0, rows // NORM_CHUNK, body, 0)


def _norm_kernel(x_ref, g_ref, h_ref):
    _norm_into(x_ref, g_ref, h_ref, NORM_TM)


def _norm(x, ln, l):
    tm = NORM_TM
    return pl.pallas_call(
        _norm_kernel,
        grid=(ROWS // tm,),
        in_specs=[pl.BlockSpec((tm, D_MODEL), lambda i: (i, 0)), _layer_vec(D_MODEL, l)],
        out_specs=pl.BlockSpec((tm, D_MODEL), lambda i: (i, 0)),
        out_shape=jax.ShapeDtypeStruct((ROWS, D_MODEL), BF16),
        compiler_params=_cparams(("parallel",), 32),
        name="input_norm",
    )(x, ln)


PROJ_TM, PROJ_TN = 2048, 1024


def _proj_kernel(h_ref, w_ref, o_ref):
    o_ref[...] = _dot(h_ref[...], w_ref[...])


def _proj(h, w, l):
    tm, tn = PROJ_TM, PROJ_TN
    return pl.pallas_call(
        _proj_kernel,
        grid=(ROWS // tm, PROJ_W // tn),
        in_specs=[
            pl.BlockSpec((tm, D_MODEL), lambda i, j: (i, 0)),
            pl.BlockSpec((None, D_MODEL, tn), lambda i, j: (l, 0, j)),
        ],
        out_specs=pl.BlockSpec((tm, tn), lambda i, j: (i, j)),
        out_shape=jax.ShapeDtypeStruct((ROWS, PROJ_W), F32),
        compiler_params=_cparams(("parallel", "arbitrary"), 52),
        name="in_proj",
    )(h, w)


ATT_HEADS = 4
SB_TK = 128
SB_GROUP = 4
SB_TQ = SB_GROUP * SB_TK


def _sb_weights(z, tri, carry, masks):
    nblk = len(masks)
    ls_l, r_l = [], []
    for j in range(nblk - 1, -1, -1):
        zj = z[:, j * SB_TK:(j + 1) * SB_TK]
        m = jnp.maximum(zj, 0.0) + jnp.log2(1.0 + jnp.exp2(-jnp.abs(zj)))
        ls_l.append(zj - m)
        if masks[j] is not None:
            m = jnp.where(masks[j], m, 0.0)
        r_l.append(_dot(m.astype(BF16), tri))
    ws = []
    for j, ls, r in zip(range(nblk - 1, -1, -1), ls_l, r_l):
        w = jnp.exp2(ls + r[:, :SB_TK] + carry)
        if masks[j] is not None:
            w = jnp.where(masks[j], w, 0.0)
        ws.append(w.astype(BF16))
        carry = carry + r[:, SB_TK:]
    return carry, jnp.concatenate(ws[::-1], axis=1)


def _sb_kernel(q_ref, k_ref, v_ref, tri_ref, o_ref, kb_ref, vb_ref):
    i = pl.program_id(2)

    @pl.when(i == 0)
    def _():
        kb_ref[...] = k_ref[...].astype(BF16)
        vb_ref[...] = v_ref[...].astype(BF16)

    tri = tri_ref[...]
    heads = [slice(hh * HEAD_DIM, (hh + 1) * HEAD_DIM) for hh in range(ATT_HEADS)]
    qs = [(q_ref[:, hd] * (HEAD_DIM ** -0.5 * LOG2E)).astype(BF16) for hd in heads]
    half, hblk = SB_TQ // 2, SB_GROUP // 2
    row = lax.broadcasted_iota(jnp.int32, (half, SB_TK), 0)
    col = lax.broadcasted_iota(jnp.int32, (half, SB_TK), 1)
    causal = [(col + j * SB_TK) < row for j in range(hblk)]
    k_diag = pl.multiple_of(i * SB_TQ, SB_TQ)

    def part(q, hd, k0, nkeys, carry, masks):
        carry, w = _sb_weights(_dot_nt(q, kb_ref[pl.ds(k0, nkeys), hd]), tri, carry, masks)
        return carry, _dot(w, vb_ref[pl.ds(k0, nkeys), hd])

    def diagonal(q, hd):
        zero = jnp.zeros((half, SB_TK), F32)
        top = part(q[:half], hd, k_diag, half, zero, causal)
        bot = part(q[half:], hd, k_diag, SB_TQ, zero, [None] * hblk + causal)
        return tuple(jnp.concatenate([t, b], axis=0) for t, b in zip(top, bot))

    def group(g, state):
        k0 = pl.multiple_of(g * SB_TQ, SB_TQ)
        out = []
        for q, hd, (carry, acc) in zip(qs, heads, state):
            carry, pv = part(q, hd, k0, SB_TQ, carry, [None] * SB_GROUP)
            out.append((carry, acc + pv))
        return tuple(out)

    state = tuple(diagonal(q, hd) for q, hd in zip(qs, heads))
    state = lax.fori_loop(0, i, lambda t, s: group(i - 1 - t, s), state)
    for hd, (_, acc) in zip(heads, state):
        o_ref[:, hd] = acc.astype(o_ref.dtype)


def _sb_tri():
    j = np.arange(SB_TK)
    t = (j[:, None] > j[None, :]).astype(np.float32)
    return jnp.asarray(-np.concatenate([t, np.ones((SB_TK, SB_TK), np.float32)], axis=1), dtype=BF16)


def _sb_attention(proj):
    nq = SEQ // SB_TQ
    hw = ATT_HEADS * HEAD_DIM
    hs = HEADS // ATT_HEADS
    return pl.pallas_call(
        _sb_kernel,
        grid=(BATCH, hs, nq),
        in_specs=[
            pl.BlockSpec((SB_TQ, hw), lambda b, h, i: (b * nq + i, G_SBQ * hs + h)),
            pl.BlockSpec((SEQ, hw), lambda b, h, i: (b, G_SBK * hs + h)),
            pl.BlockSpec((SEQ, hw), lambda b, h, i: (b, G_SBV * hs + h)),
            pl.BlockSpec((SB_TK, 2 * SB_TK), lambda b, h, i: (0, 0)),
        ],
        out_specs=pl.BlockSpec((SB_TQ, hw), lambda b, h, i: (b * nq + i, h)),
        out_shape=jax.ShapeDtypeStruct((ROWS, BRANCH_WIDTH), BF16),
        scratch_shapes=[pltpu.VMEM((SEQ, hw), BF16), pltpu.VMEM((SEQ, hw), BF16)],
        compiler_params=_cparams(("parallel", "parallel", "arbitrary"), 32),
        name="sb_attention",
    )(proj, proj, proj, _sb_tri())


MLA_TM = 1024
MLA_SCALE = (MLA_NOPE + MLA_ROPE) ** -0.5 * LOG2E


def _rope64(x, cos, sin_signed):
    lane = lax.broadcasted_iota(jnp.int32, x.shape, 1)
    first = (lane % MLA_ROPE) < (MLA_ROPE // 2)
    rot = jnp.where(first, pltpu.roll(x, LANES - MLA_ROPE // 2, 1), pltpu.roll(x, MLA_ROPE // 2, 1))
    return x * cos + rot * sin_signed


def _mla_prep_kernel(c_ref, pe_ref, gain_ref, w_ref, cos_ref, sin_ref,
                     kn_ref, v_ref, qpe_ref, kpe_ref):
    cn = _rmsnorm_rows(c_ref[...], gain_ref[...]).astype(BF16)
    kv = _dot(cn, w_ref[...])
    kn_ref[...] = kv[:, :GROUP].astype(BF16)
    v_ref[...] = kv[:, GROUP:].astype(BF16)
    cos = cos_ref[...]
    sin = sin_ref[...]
    lane = lax.broadcasted_iota(jnp.int32, (MLA_TM, LANES), 1)
    low = lane < MLA_ROPE
    for c in range(2):
        r = _rope64(pe_ref[:, c * LANES:(c + 1) * LANES], cos, sin) * MLA_SCALE
        qpe_ref[:, (2 * c) * LANES:(2 * c + 1) * LANES] = jnp.where(low, r, 0.0).astype(BF16)
        qpe_ref[:, (2 * c + 1) * LANES:(2 * c + 2) * LANES] = jnp.where(
            low, pltpu.roll(r, MLA_ROPE, 1), 0.0).astype(BF16)
    r = _rope64(pe_ref[:, 2 * LANES:3 * LANES], cos, sin)
    kpe_ref[...] = jnp.where(low, r, 0.0).astype(BF16)


def _mla_prep(proj, gain, w_ukv, cos, sin, l):
    tm = MLA_TM
    row512 = pl.BlockSpec((tm, GROUP), lambda i: (i, 0))
    row128 = pl.BlockSpec((tm, LANES), lambda i: (i, 0))
    return pl.pallas_call(
        _mla_prep_kernel,
        grid=(ROWS // tm,),
        in_specs=[
            pl.BlockSpec((tm, GROUP), lambda i: (i, G_CKV)),
            pl.BlockSpec((tm, GROUP), lambda i: (i, G_PE)),
            _layer_vec(MLA_KV_RANK, l),
            pl.BlockSpec((None, MLA_KV_RANK, 2 * GROUP), lambda i: (l, 0, 0)),
            row128, row128,
        ],
        out_specs=[row512, row512, row512, row128],
        out_shape=[
            jax.ShapeDtypeStruct((ROWS, GROUP), BF16),
            jax.ShapeDtypeStruct((ROWS, GROUP), BF16),
            jax.ShapeDtypeStruct((ROWS, GROUP), BF16),
            jax.ShapeDtypeStruct((ROWS, LANES), BF16),
        ],
        compiler_params=_cparams(("parallel",), 40),
        name="mla_prep",
    )(proj, proj, gain, w_ukv, cos, sin)


MLA_T = 512


def _mla_step(q, k, v, m, l, acc, mask):
    s = _dot_nt(q, k)
    if mask is not None:
        s = jnp.where(mask, s, -jnp.inf)
    m_new = jnp.maximum(m, jnp.max(s, axis=-1, keepdims=True))
    alpha = jnp.exp2(m - m_new)
    p = jnp.exp2(s - m_new)
    l = alpha * l + jnp.sum(p, axis=-1, keepdims=True)
    acc = alpha * acc + _dot(p.astype(BF16), v)
    return m_new, l, acc


def _mla_kernel(qn_ref, qpe_ref, kn_ref, kpe_ref, v_ref, o_ref, kc_ref):
    i = pl.program_id(2)
    heads = [slice(hh * HEAD_DIM, (hh + 1) * HEAD_DIM) for hh in range(ATT_HEADS)]

    @pl.when(i == 0)
    def _():
        for hh, hd in enumerate(heads):
            kc_ref[hh, :, :LANES] = kn_ref[:, hd]
            kc_ref[hh, :, LANES:] = kpe_ref[...]

    qs = [jnp.concatenate([(qn_ref[:, hd] * MLA_SCALE).astype(BF16), qpe_ref[:, hd]], axis=1)
          for hd in heads]
    row = lax.broadcasted_iota(jnp.int32, (MLA_T, MLA_T), 0)
    col = lax.broadcasted_iota(jnp.int32, (MLA_T, MLA_T), 1)

    def block(g, state, mask):
        k0 = pl.multiple_of(g * MLA_T, MLA_T)
        return tuple(
            _mla_step(q, kc_ref[hh, pl.ds(k0, MLA_T), :], v_ref[pl.ds(k0, MLA_T), hd], m, l, acc, mask)
            for hh, (q, hd, (m, l, acc)) in enumerate(zip(qs, heads, state)))

    init = (jnp.full((MLA_T, 1), -jnp.inf, F32), jnp.zeros((MLA_T, 1), F32), jnp.zeros((MLA_T, HEAD_DIM), F32))
    state = block(i, (init,) * ATT_HEADS, col <= row)
    state = lax.fori_loop(0, i, lambda t, s: block(t, s, None), state)
    for hd, (_, l, acc) in zip(heads, state):
        o_ref[:, hd] = (acc / l).astype(o_ref.dtype)


def _mla_attention(proj, kn, v, qpe, kpe):
    nq = SEQ // MLA_T
    hw = ATT_HEADS * HEAD_DIM
    hs = HEADS // ATT_HEADS
    return pl.pallas_call(
        _mla_kernel,
        grid=(BATCH, hs, nq),
        in_specs=[
            pl.BlockSpec((MLA_T, hw), lambda b, h, i: (b * nq + i, G_MQN * hs + h)),
            pl.BlockSpec((MLA_T, hw), lambda b, h, i: (b * nq + i, h)),
            pl.BlockSpec((SEQ, hw), lambda b, h, i: (b, h)),
            pl.BlockSpec((SEQ, LANES), lambda b, h, i: (b, 0)),
            pl.BlockSpec((SEQ, hw), lambda b, h, i: (b, h)),
        ],
        out_specs=pl.BlockSpec((MLA_T, hw), lambda b, h, i: (b * nq + i, h)),
        out_shape=jax.ShapeDtypeStruct((ROWS, BRANCH_WIDTH), BF16),
        scratch_shapes=[pltpu.VMEM((ATT_HEADS, SEQ, 2 * LANES), BF16)],
        compiler_params=_cparams(("parallel", "parallel", "arbitrary"), 32),
        name="mla_attention",
    )(proj, qpe, kn, kpe, v)


def _rope128(x, cos, sin_signed):
    return x * cos + pltpu.roll(x, HEAD_DIM // 2, 1) * sin_signed


def _ret_kernel(q_ref, k_ref, v_ref, g_ref, cos_ref, sin_ref, dec_ref, qd_ref, kd_ref, cd_ref, o_ref):
    C = RET_CHUNK
    dec = dec_ref[...]
    qd = qd_ref[...]
    kd = kd_ref[...]
    cd = cd_ref[...]
    state = jnp.zeros((HEAD_DIM, HEAD_DIM), F32)
    for n in range(SEQ // C):
        sl = pl.ds(n * C, C)
        cos = cos_ref[sl, :]
        sin = sin_ref[sl, :]
        q = _rope128(q_ref[sl, :], cos, sin)
        k = _rope128(k_ref[sl, :], cos, sin) * (HEAD_DIM ** -0.5)
        v = v_ref[sl, :].astype(BF16)
        scores = _dot_nt(q.astype(BF16), k.astype(BF16)) * dec
        inner = _dot(scores.astype(BF16), v)
        cross = _dot((q * qd).astype(BF16), state.astype(BF16))
        state = state * cd + _dot_tn((k * kd).astype(BF16), v)
        o = inner + cross
        o = o * lax.rsqrt(jnp.mean(o * o, axis=-1, keepdims=True) + EPS)
        g = g_ref[sl, :]
        o_ref[sl, :] = (o * (g * jax.nn.sigmoid(g))).astype(o_ref.dtype)


def _ret_consts():
    C = RET_CHUNK
    log_gamma = jnp.log1p(-(2.0 ** (-5.0 - jnp.arange(HEADS, dtype=F32))))
    idx = jnp.arange(C, dtype=F32)
    diff = idx[:, None] - idx[None, :]
    decay = jnp.where(diff >= 0, jnp.exp(log_gamma[:, None, None] * jnp.maximum(diff, 0.0)), 0.0)
    k_decay = jnp.exp(log_gamma[:, None] * (C - 1.0 - idx)[None, :])
    q_decay = jnp.exp(log_gamma[:, None] * (idx + 1.0)[None, :])
    chunk_decay = jnp.exp(log_gamma * C)
    bc = lambda t: jnp.broadcast_to(t[:, :, None], (HEADS, C, HEAD_DIM))
    cdm = jnp.broadcast_to(chunk_decay[:, None, None], (HEADS, HEAD_DIM, HEAD_DIM))
    return decay, bc(q_decay), bc(k_decay), cdm


def _retention(proj, cos, sin, consts):
    seq = lambda g: pl.BlockSpec((SEQ, HEAD_DIM), lambda b, h: (b, g * HB + h))
    tab = pl.BlockSpec((SEQ, LANES), lambda b, h: (b, 0))
    cst = pl.BlockSpec((None, HEAD_DIM, HEAD_DIM), lambda b, h: (h, 0, 0))
    return pl.pallas_call(
        _ret_kernel,
        grid=(BATCH, HEADS),
        in_specs=[seq(G_RQ), seq(G_RK), seq(G_RV), seq(G_RG), tab, tab, cst, cst, cst, cst],
        out_specs=pl.BlockSpec((SEQ, HEAD_DIM), lambda b, h: (b, h)),
        out_shape=jax.ShapeDtypeStruct((ROWS, BRANCH_WIDTH), BF16),
        compiler_params=_cparams(("parallel", "parallel"), 32),
        name="retention",
    )(proj, proj, proj, proj, cos, sin, *consts)


LRU_TS = 256
LRU_PAD = 8


def _lru_kernel(x_ref, y_ref, cw_ref, cb_ref, wa_ref, ba_ref, wx_ref, bx_ref, lam_ref, o_ref,
                xp_ref, p_ref, q_ref):
    W = LRU_WIDTH
    xp_ref[0:LRU_PAD, :] = jnp.zeros((LRU_PAD, W), F32)
    cw = cw_ref[...]
    cb = cb_ref[...]
    ba = ba_ref[...]
    bx = bx_ref[...]
    nlam = -lam_ref[...]
    sp = jnp.maximum(nlam, 0.0) + jnp.log1p(jnp.exp(-jnp.abs(nlam)))
    sub = lax.broadcasted_iota(jnp.int32, (LRU_TS // SUBLANES, SUBLANES, W), 1)

    def copy(c, _):
        r0 = pl.multiple_of(c * LRU_TS, LRU_TS)
        xp_ref[pl.ds(LRU_PAD + r0, LRU_TS), :] = x_ref[pl.ds(r0, LRU_TS), :]
        return 0

    lax.fori_loop(0, SEQ // LRU_TS, copy, 0)

    def gates(c, _):
        r0 = pl.multiple_of(c * LRU_TS, LRU_TS)
        win = xp_ref[pl.ds(r0, LRU_TS + LRU_PAD), :].reshape(LRU_TS // SUBLANES + 1, SUBLANES, W)
        xc = cb
        for tap in range(CONV_WIDTH):
            s = CONV_WIDTH - 1 - tap
            rolled = pltpu.roll(win, s, 1) if s else win
            shifted = jnp.where(sub >= s, rolled[1:], rolled[:-1]) if s else win[1:]
            xc = xc + shifted * cw[tap:tap + 1, :]
        xc = xc.reshape(LRU_TS, W)
        ra, ri = [], []
        for blk in range(LRU_BLOCKS):
            xb = xc[:, blk * LRU_BLOCK:(blk + 1) * LRU_BLOCK].astype(BF16)
            ra.append(_dot(xb, wa_ref[blk]))
            ri.append(_dot(xb, wx_ref[blk]))
        r = jax.nn.sigmoid(jnp.concatenate(ra, axis=1) + ba)
        gi = jax.nn.sigmoid(jnp.concatenate(ri, axis=1) + bx)
        log_a = -LRU_C * r * sp
        a = jnp.exp(log_a)
        t = jnp.tanh(log_a)
        b = jnp.sqrt(-2.0 * t / (1.0 - t)) * (gi * xc)
        a = a.reshape(LRU_TS // SUBLANES, SUBLANES, W)
        b = b.reshape(LRU_TS // SUBLANES, SUBLANES, W)
        for k in (1, 2, 4):
            keep = sub >= k
            b = jnp.where(keep, b + a * pltpu.roll(b, k, 1), b)
            a = jnp.where(keep, a * pltpu.roll(a, k, 1), a)
        p_ref[pl.ds(r0, LRU_TS), :] = a.reshape(LRU_TS, W)
        q_ref[pl.ds(r0, LRU_TS), :] = b.reshape(LRU_TS, W)
        return 0

    lax.fori_loop(0, SEQ // LRU_TS, gates, 0)

    def carry(gidx, h):
        r0 = pl.multiple_of(gidx * SUBLANES, SUBLANES)
        rows = p_ref[pl.ds(r0, SUBLANES), :] * h + q_ref[pl.ds(r0, SUBLANES), :]
        q_ref[pl.ds(r0, SUBLANES), :] = rows
        return jnp.broadcast_to(rows[SUBLANES - 1:SUBLANES, :], (SUBLANES, W))

    lax.fori_loop(0, SEQ // SUBLANES, carry, jnp.zeros((SUBLANES, W), F32), unroll=8)

    def out(c, _):
        r0 = pl.multiple_of(c * LRU_TS, LRU_TS)
        y = y_ref[pl.ds(r0, LRU_TS), :]
        o_ref[pl.ds(r0, LRU_TS), :] = (q_ref[pl.ds(r0, LRU_TS), :] * jax.nn.gelu(y)).astype(o_ref.dtype)
        return 0

    lax.fori_loop(0, SEQ // LRU_TS, out, 0)


def _rglru(proj, conv_w, conv_b, w_a, b_a, w_x, b_x, lam, l):
    W = LRU_WIDTH
    vec = _layer_vec(W, l)
    blk = pl.BlockSpec((None, LRU_BLOCKS, LRU_BLOCK, LRU_BLOCK), lambda b: (l, 0, 0, 0))
    return pl.pallas_call(
        _lru_kernel,
        grid=(BATCH,),
        in_specs=[
            pl.BlockSpec((SEQ, W), lambda b: (b, G_LX)),
            pl.BlockSpec((SEQ, W), lambda b: (b, G_LY)),
            pl.BlockSpec((None, CONV_WIDTH, W), lambda b: (l, 0, 0)),
            vec, blk, vec, blk, vec, vec,
        ],
        out_specs=pl.BlockSpec((SEQ, W), lambda b: (b, 0)),
        out_shape=jax.ShapeDtypeStruct((ROWS, W), BF16),
        scratch_shapes=[
            pltpu.VMEM((SEQ + LRU_PAD, W), F32),
            pltpu.VMEM((SEQ, W), F32),
            pltpu.VMEM((SEQ, W), F32),
        ],
        compiler_params=_cparams(("parallel",), 48),
        name="rglru",
    )(proj, proj, conv_w, conv_b, w_a, b_a, w_x, b_x, lam)


MERGE_TM, MERGE_TN = 1024, 512


def _merge_kernel(h_ref, o0_ref, o1_ref, o2_ref, o3_ref, wg_ref, wb_ref, out_ref):
    h = h_ref[...]
    acc = None
    for br, o_ref in enumerate((o0_ref, o1_ref, o2_ref, o3_ref)):
        t = jax.nn.sigmoid(_dot(h, wg_ref[br])) * _dot(o_ref[...], wb_ref[br])
        acc = t if acc is None else acc + t
    out_ref[...] = acc.astype(out_ref.dtype)


def _merge(h, branches, w_gate, w_branch, l):
    tm, tn = MERGE_TM, MERGE_TN
    obs = pl.BlockSpec((tm, BRANCH_WIDTH), lambda i, j: (i, 0))
    return pl.pallas_call(
        _merge_kernel,
        grid=(ROWS // tm, D_MODEL // tn),
        in_specs=[
            pl.BlockSpec((tm, D_MODEL), lambda i, j: (i, 0)),
            obs, obs, obs, obs,
            pl.BlockSpec((None, N_BRANCH, D_MODEL, tn), lambda i, j: (l, 0, 0, j)),
            pl.BlockSpec((None, N_BRANCH, BRANCH_WIDTH, tn), lambda i, j: (l, 0, 0, j)),
        ],
        out_specs=pl.BlockSpec((tm, tn), lambda i, j: (i, j)),
        out_shape=jax.ShapeDtypeStruct((ROWS, D_MODEL), BF16),
        compiler_params=_cparams(("parallel", "arbitrary"), 56),
        name="gated_merge",
    )(h, *branches, w_gate, w_branch)


OUT_TM, OUT_TN = 512, D_MODEL


def _outproj_kernel(m_ref, w_ref, x_ref, o_ref):
    o_ref[...] = x_ref[...] + _dot(m_ref[...], w_ref[...])


def _outproj(mixed, w_out, x, l):
    tm, tn = OUT_TM, OUT_TN
    return pl.pallas_call(
        _outproj_kernel,
        grid=(ROWS // tm, D_MODEL // tn),
        in_specs=[
            pl.BlockSpec((tm, D_MODEL), lambda i, j: (i, 0)),
            pl.BlockSpec((None, D_MODEL, tn), lambda i, j: (l, 0, j)),
            pl.BlockSpec((tm, tn), lambda i, j: (i, j)),
        ],
        out_specs=pl.BlockSpec((tm, tn), lambda i, j: (i, j)),
        out_shape=jax.ShapeDtypeStruct((ROWS, D_MODEL), F32),
        compiler_params=_cparams(("parallel", "arbitrary"), 44),
        name="out_proj",
    )(mixed, w_out, x)


FFN_TM, FFN_TF = 1024, 512
FFN_STEPS = pl.cdiv(D_FF, FFN_TF)
FFN_TAIL = D_FF - (FFN_STEPS - 1) * FFN_TF


def _ffn_kernel(x_ref, g_ref, wg_ref, wu_ref, wd_ref, gn_ref, *rest, final):
    if final:
        o_ref, h_ref = rest
        n_ref = o_ref
    else:
        o_ref, n_ref = rest
        h_ref = n_ref
    f = pl.program_id(1)

    @pl.when(f == 0)
    def _():
        _norm_into(x_ref, g_ref, h_ref, FFN_TM)
        o_ref[...] = x_ref[...]

    def hidden_tile(width):
        h = h_ref[...]
        a = _dot(h, wg_ref[:, :width])
        act = (a * jax.nn.sigmoid(a) * _dot(h, wu_ref[:, :width])).astype(BF16)
        o_ref[...] += _dot(act, wd_ref[:width, :])

    pl.when(f < FFN_STEPS - 1)(lambda: hidden_tile(FFN_TF))

    @pl.when(f == FFN_STEPS - 1)
    def _():
        hidden_tile(FFN_TAIL)
        _norm_into(o_ref, gn_ref, n_ref, FFN_TM)


def _ffn(x, ln, wg, wu, wd, l, ln_next, l_next, final):
    tm, tf = FFN_TM, FFN_TF
    row = pl.BlockSpec((tm, D_MODEL), lambda i, f: (i, 0))
    out_f32 = jax.ShapeDtypeStruct((ROWS, D_MODEL), F32)
    return pl.pallas_call(
        functools.partial(_ffn_kernel, final=final),
        grid=(ROWS // tm, FFN_STEPS),
        in_specs=[
            row,
            _layer_vec(D_MODEL, l),
            pl.BlockSpec((None, D_MODEL, tf), lambda i, f: (l, 0, f)),
            pl.BlockSpec((None, D_MODEL, tf), lambda i, f: (l, 0, f)),
            pl.BlockSpec((None, tf, D_MODEL), lambda i, f: (l, f, 0)),
            _layer_vec(D_MODEL, l_next),
        ],
        out_specs=row if final else [row, row],
        out_shape=out_f32 if final else [out_f32, jax.ShapeDtypeStruct((ROWS, D_MODEL), BF16)],
        scratch_shapes=[pltpu.VMEM((tm, D_MODEL), BF16)] if final else [],
        compiler_params=_cparams(("parallel", "arbitrary"), 62),
        name="ffn_final" if final else "ffn",
    )(x, ln, wg, wu, wd, ln_next)


def _pack_w_in(w):
    w = w.astype(BF16)
    lead = w.shape[:-1]
    mq0 = 3 * GROUP
    mq = w[..., mq0:mq0 + HEADS * (MLA_NOPE + MLA_ROPE)].reshape(*lead, HEADS, MLA_NOPE + MLA_ROPE)
    ckv0 = mq0 + HEADS * (MLA_NOPE + MLA_ROPE)
    kr0 = ckv0 + MLA_KV_RANK
    rest0 = kr0 + MLA_ROPE
    pe_pad = GROUP - HEADS * MLA_ROPE - MLA_ROPE
    parts = [
        w[..., :mq0],
        mq[..., :MLA_NOPE].reshape(*lead, HEADS * MLA_NOPE),
        w[..., ckv0:kr0],
        mq[..., MLA_NOPE:].reshape(*lead, HEADS * MLA_ROPE),
        w[..., kr0:rest0],
        jnp.zeros((*lead, pe_pad), w.dtype),
        w[..., rest0:],
    ]
    return jnp.concatenate(parts, axis=-1)


def _pack_w_ukv(w):
    w = w.astype(BF16)
    lead = w.shape[:-1]
    t = w.reshape(*lead, HEADS, 2 * HEAD_DIM)
    return jnp.concatenate([t[..., :HEAD_DIM].reshape(*lead, GROUP), t[..., HEAD_DIM:].reshape(*lead, GROUP)],
                           axis=-1)


def kernel(x, positions, ln1, w_in, mla_kv_gain, mla_w_ukv, lru_conv_w, lru_conv_b, lru_w_a, lru_b_a,
           lru_w_x, lru_b_x, lru_lambda, w_branch, w_gate, w_out, ln2, ffn_w_gate, ffn_w_up,
           ffn_w_down, ln_final):
    ret_cos, ret_sin, mla_cos, mla_sin = _rope_tables(positions)
    ret_consts = _ret_consts()
    vec = lambda t: t.reshape(DEPTH, 1, -1)
    w_in_p = _pack_w_in(w_in)
    w_ukv_p = _pack_w_ukv(mla_w_ukv)
    w_a, w_x = lru_w_a.astype(BF16), lru_w_x.astype(BF16)
    w_gate_b, w_branch_b, w_out_b = w_gate.astype(BF16), w_branch.astype(BF16), w_out.astype(BF16)
    wf_gate, wf_up, wf_down = ffn_w_gate.astype(BF16), ffn_w_up.astype(BF16), ffn_w_down.astype(BF16)
    ln1, ln2, kv_gain = vec(ln1), vec(ln2), vec(mla_kv_gain)
    conv_b, b_a, b_x, lam = vec(lru_conv_b), vec(lru_b_a), vec(lru_b_x), vec(lru_lambda)

    xf = x.reshape(ROWS, D_MODEL)
    h = _norm(xf, ln1, 0)
    for l in range(DEPTH):
        proj = _proj(h, w_in_p, l)
        o_sb = _sb_attention(proj)
        kn, v, qpe, kpe = _mla_prep(proj, kv_gain, w_ukv_p, mla_cos, mla_sin, l)
        o_mla = _mla_attention(proj, kn, v, qpe, kpe)
        o_ret = _retention(proj, ret_cos, ret_sin, ret_consts)
        o_lru = _rglru(proj, lru_conv_w, conv_b, w_a, b_a, w_x, b_x, lam, l)
        mixed = _merge(h, (o_sb, o_mla, o_ret, o_lru), w_gate_b, w_branch_b, l)
        xf = _outproj(mixed, w_out_b, xf, l)
        if l < DEPTH - 1:
            xf, h = _ffn(xf, ln2, wf_gate, wf_up, wf_down, l, ln1, l + 1, final=False)
        else:
            xf = _ffn(xf, ln2, wf_gate, wf_up, wf_down, l, ln_final.reshape(1, 1, D_MODEL), 0, final=True)
    return xf.reshape(BATCH, SEQ, D_MODEL)
```

```python
import functools
import math

import numpy as np
import jax
import jax.numpy as jnp
from jax import lax
from jax.experimental import pallas as pl
from jax.experimental.pallas import tpu as pltpu

F32 = jnp.float32
BF16 = jnp.bfloat16

D_MODEL = 2048
BATCH = 8
SEQ = 2048
DEPTH = 4
ROWS = BATCH * SEQ
HEAD_DIM = 128
HEADS = 4
ROPE_THETA = 10000.0
EPS = 1e-6
N_BRANCH = 4
BRANCH_WIDTH = 512
MLA_NOPE = 128
MLA_ROPE = 64
MLA_KV_RANK = 512
RET_CHUNK = 128
LRU_WIDTH = 512
LRU_BLOCKS = 4
LRU_BLOCK = 128
CONV_WIDTH = 4
LRU_C = 8.0
D_FF = 5632

LANES = 128
SUBLANES = 8
GROUP = 512
G_SBQ, G_SBK, G_SBV, G_MQN, G_CKV, G_PE, G_RQ, G_RK, G_RV, G_RG, G_LX, G_LY = range(12)
PROJ_W = 12 * GROUP
HB = GROUP // LANES

VMEM_MB = 1024 * 1024
LOG2E = math.log2(math.e)


def _cparams(sem, vmem_mb):
    return pltpu.CompilerParams(dimension_semantics=sem, vmem_limit_bytes=vmem_mb * VMEM_MB)


def _dot(a, b):
    return jnp.dot(a, b, preferred_element_type=F32)


def _dot_nt(a, b):
    return lax.dot_general(a, b, (((1,), (1,)), ((), ())), preferred_element_type=F32)


def _dot_tn(a, b):
    return lax.dot_general(a, b, (((0,), (0,)), ((), ())), preferred_element_type=F32)


def _rmsnorm_rows(x, g):
    return x * lax.rsqrt(jnp.mean(x * x, axis=-1, keepdims=True) + EPS) * g


def _layer_vec(width, l):
    return pl.BlockSpec((None, 1, width), lambda *_: (l, 0, 0))


def _tables_kernel(pos_ref, f_ref, sr_ref, sm_ref, rc_ref, rs_ref, mc_ref, ms_ref):
    ang = pos_ref[...] * f_ref[...]
    low = lax.broadcasted_iota(jnp.int32, ang.shape, 1) < LANES // 2
    c = jnp.cos(ang)
    s = jnp.sin(ang)
    c_rot = pltpu.roll(c, LANES // 2, 1)
    s_rot = pltpu.roll(s, LANES // 2, 1)
    rc_ref[...] = jnp.where(low, c, c_rot)
    rs_ref[...] = jnp.where(low, s, s_rot) * sr_ref[...]
    mc_ref[...] = jnp.where(low, c_rot, c)
    ms_ref[...] = jnp.where(low, s_rot, s) * sm_ref[...]


def _rope_tables(positions):
    tm = 1024
    pos = jnp.broadcast_to(positions.reshape(ROWS, 1).astype(F32), (ROWS, LANES))

    def freq(d):
        return ROPE_THETA ** (-jnp.arange(0, d, 2, dtype=F32) / d)

    def sign(d):
        return jnp.concatenate([-jnp.ones((d // 2,), F32), jnp.ones((d // 2,), F32)])

    f = jnp.concatenate([freq(HEAD_DIM), jnp.tile(freq(MLA_ROPE), 2)]).reshape(1, LANES)
    sr = sign(HEAD_DIM).reshape(1, LANES)
    sm = jnp.tile(sign(MLA_ROPE), 2).reshape(1, LANES)
    row = pl.BlockSpec((tm, LANES), lambda i: (i, 0))
    vec = pl.BlockSpec((1, LANES), lambda i: (0, 0))
    shp = jax.ShapeDtypeStruct((ROWS, LANES), F32)
    return pl.pallas_call(
        _tables_kernel,
        grid=(ROWS // tm,),
        in_specs=[row, vec, vec, vec],
        out_specs=[row, row, row, row],
        out_shape=[shp, shp, shp, shp],
        compiler_params=_cparams(("parallel",), 32),
        name="rope_tables",
    )(pos, f, sr, sm)


NORM_TM = 512
NORM_CHUNK = 256


def _norm_into(x_ref, g_ref, h_ref, rows):
    g = g_ref[...]

    def body(c, _):
        r0 = pl.multiple_of(c * NORM_CHUNK, NORM_CHUNK)
        x = x_ref[pl.ds(r0, NORM_CHUNK), :]
        h_ref[pl.ds(r0, NORM_CHUNK), :] = _rmsnorm_rows(x, g).astype(h_ref.dtype)
        return 0

    lax.fori_loop(0, rows // NORM_CHUNK, body, 0)


def _norm_kernel(x_ref, g_ref, h_ref):
    _norm_into(x_ref, g_ref, h_ref, NORM_TM)


def _norm(x, ln, l):
    tm = NORM_TM
    return pl.pallas_call(
        _norm_kernel,
        grid=(ROWS // tm,),
        in_specs=[pl.BlockSpec((tm, D_MODEL), lambda i: (i, 0)), _layer_vec(D_MODEL, l)],
        out_specs=pl.BlockSpec((tm, D_MODEL), lambda i: (i, 0)),
        out_shape=jax.ShapeDtypeStruct((ROWS, D_MODEL), BF16),
        compiler_params=_cparams(("parallel",), 32),
        name="input_norm",
    )(x, ln)


PROJ_TM, PROJ_TN = 2048, 1024


def _proj_kernel(h_ref, w_ref, o_ref):
    o_ref[...] = _dot(h_ref[...], w_ref[...])


def _proj(h, w, l):
    tm, tn = PROJ_TM, PROJ_TN
    return pl.pallas_call(
        _proj_kernel,
        grid=(ROWS // tm, PROJ_W // tn),
        in_specs=[
            pl.BlockSpec((tm, D_MODEL), lambda i, j: (i, 0)),
            pl.BlockSpec((None, D_MODEL, tn), lambda i, j: (l, 0, j)),
        ],
        out_specs=pl.BlockSpec((tm, tn), lambda i, j: (i, j)),
        out_shape=jax.ShapeDtypeStruct((ROWS, PROJ_W), F32),
        compiler_params=_cparams(("parallel", "arbitrary"), 52),
        name="in_proj",
    )(h, w)


ATT_HEADS = 4
SB_TK = 128
SB_GROUP = 4
SB_TQ = SB_GROUP * SB_TK


def _sb_weights(z, tri, carry, masks):
    nblk = len(masks)
    ls_l, r_l = [], []
    for j in range(nblk - 1, -1, -1):
        zj = z[:, j * SB_TK:(j + 1) * SB_TK]
        m = jnp.maximum(zj, 0.0) + jnp.log2(1.0 + jnp.exp2(-jnp.abs(zj)))
        ls_l.append(zj - m)
        if masks[j] is not None:
            m = jnp.where(masks[j], m, 0.0)
        r_l.append(_dot(m.astype(BF16), tri))
    ws = []
    for j, ls, r in zip(range(nblk - 1, -1, -1), ls_l, r_l):
        w = jnp.exp2(ls + r[:, :SB_TK] + carry)
        if masks[j] is not None:
            w = jnp.where(masks[j], w, 0.0)
        ws.append(w.astype(BF16))
        carry = carry + r[:, SB_TK:]
    return carry, jnp.concatenate(ws[::-1], axis=1)


def _sb_kernel(q_ref, k_ref, v_ref, tri_ref, o_ref, kb_ref, vb_ref):
    i = pl.program_id(2)

    @pl.when(i == 0)
    def _():
        kb_ref[...] = k_ref[...].astype(BF16)
        vb_ref[...] = v_ref[...].astype(BF16)

    tri = tri_ref[...]
    heads = [slice(hh * HEAD_DIM, (hh + 1) * HEAD_DIM) for hh in range(ATT_HEADS)]
    qs = [(q_ref[:, hd] * (HEAD_DIM ** -0.5 * LOG2E)).astype(BF16) for hd in heads]
    half, hblk = SB_TQ // 2, SB_GROUP // 2
    row = lax.broadcasted_iota(jnp.int32, (half, SB_TK), 0)
    col = lax.broadcasted_iota(jnp.int32, (half, SB_TK), 1)
    causal = [(col + j * SB_TK) < row for j in range(hblk)]
    k_diag = pl.multiple_of(i * SB_TQ, SB_TQ)

    def part(q, hd, k0, nkeys, carry, masks):
        carry, w = _sb_weights(_dot_nt(q, kb_ref[pl.ds(k0, nkeys), hd]), tri, carry, masks)
        return carry, _dot(w, vb_ref[pl.ds(k0, nkeys), hd])

    def diagonal(q, hd):
        zero = jnp.zeros((half, SB_TK), F32)
        top = part(q[:half], hd, k_diag, half, zero, causal)
        bot = part(q[half:], hd, k_diag, SB_TQ, zero, [None] * hblk + causal)
        return tuple(jnp.concatenate([t, b], axis=0) for t, b in zip(top, bot))

    def group(g, state):
        k0 = pl.multiple_of(g * SB_TQ, SB_TQ)
        out = []
        for q, hd, (carry, acc) in zip(qs, heads, state):
            carry, pv = part(q, hd, k0, SB_TQ, carry, [None] * SB_GROUP)
            out.append((carry, acc + pv))
        return tuple(out)

    state = tuple(diagonal(q, hd) for q, hd in zip(qs, heads))
    state = lax.fori_loop(0, i, lambda t, s: group(i - 1 - t, s), state)
    for hd, (_, acc) in zip(heads, state):
        o_ref[:, hd] = acc.astype(o_ref.dtype)


def _sb_tri():
    j = np.arange(SB_TK)
    t = (j[:, None] > j[None, :]).astype(np.float32)
    return jnp.asarray(-np.concatenate([t, np.ones((SB_TK, SB_TK), np.float32)], axis=1), dtype=BF16)


def _sb_attention(proj):
    nq = SEQ // SB_TQ
    hw = ATT_HEADS * HEAD_DIM
    hs = HEADS // ATT_HEADS
    return pl.pallas_call(
        _sb_kernel,
        grid=(BATCH, hs, nq),
        in_specs=[
            pl.BlockSpec((SB_TQ, hw), lambda b, h, i: (b * nq + i, G_SBQ * hs + h)),
            pl.BlockSpec((SEQ, hw), lambda b, h, i: (b, G_SBK * hs + h)),
            pl.BlockSpec((SEQ, hw), lambda b, h, i: (b, G_SBV * hs + h)),
            pl.BlockSpec((SB_TK, 2 * SB_TK), lambda b, h, i: (0, 0)),
        ],
        out_specs=pl.BlockSpec((SB_TQ, hw), lambda b, h, i: (b * nq + i, h)),
        out_shape=jax.ShapeDtypeStruct((ROWS, BRANCH_WIDTH), BF16),
        scratch_shapes=[pltpu.VMEM((SEQ, hw), BF16), pltpu.VMEM((SEQ, hw), BF16)],
        compiler_params=_cparams(("parallel", "parallel", "arbitrary"), 32),
        name="sb_attention",
    )(proj, proj, proj, _sb_tri())


MLA_TM = 1024
MLA_SCALE = (MLA_NOPE + MLA_ROPE) ** -0.5 * LOG2E


def _rope64(x, cos, sin_signed):
    lane = lax.broadcasted_iota(jnp.int32, x.shape, 1)
    first = (lane % MLA_ROPE) < (MLA_ROPE // 2)
    rot = jnp.where(first, pltpu.roll(x, LANES - MLA_ROPE // 2, 1), pltpu.roll(x, MLA_ROPE // 2, 1))
    return x * cos + rot * sin_signed


def _mla_prep_kernel(c_ref, pe_ref, gain_ref, w_ref, cos_ref, sin_ref,
                     kn_ref, v_ref, qpe_ref, kpe_ref):
    cn = _rmsnorm_rows(c_ref[...], gain_ref[...]).astype(BF16)
    kv = _dot(cn, w_ref[...])
    kn_ref[...] = kv[:, :GROUP].astype(BF16)
    v_ref[...] = kv[:, GROUP:].astype(BF16)
    cos = cos_ref[...]
    sin = sin_ref[...]
    lane = lax.broadcasted_iota(jnp.int32, (MLA_TM, LANES), 1)
    low = lane < MLA_ROPE
    for c in range(2):
        r = _rope64(pe_ref[:, c * LANES:(c + 1) * LANES], cos, sin) * MLA_SCALE
        qpe_ref[:, (2 * c) * LANES:(2 * c + 1) * LANES] = jnp.where(low, r, 0.0).astype(BF16)
        qpe_ref[:, (2 * c + 1) * LANES:(2 * c + 2) * LANES] = jnp.where(
            low, pltpu.roll(r, MLA_ROPE, 1), 0.0).astype(BF16)
    r = _rope64(pe_ref[:, 2 * LANES:3 * LANES], cos, sin)
    kpe_ref[...] = jnp.where(low, r, 0.0).astype(BF16)


def _mla_prep(proj, gain, w_ukv, cos, sin, l):
    tm = MLA_TM
    row512 = pl.BlockSpec((tm, GROUP), lambda i: (i, 0))
    row128 = pl.BlockSpec((tm, LANES), lambda i: (i, 0))
    return pl.pallas_call(
        _mla_prep_kernel,
        grid=(ROWS // tm,),
        in_specs=[
            pl.BlockSpec((tm, GROUP), lambda i: (i, G_CKV)),
            pl.BlockSpec((tm, GROUP), lambda i: (i, G_PE)),
            _layer_vec(MLA_KV_RANK, l),
            pl.BlockSpec((None, MLA_KV_RANK, 2 * GROUP), lambda i: (l, 0, 0)),
            row128, row128,
        ],
        out_specs=[row512, row512, row512, row128],
        out_shape=[
            jax.ShapeDtypeStruct((ROWS, GROUP), BF16),
            jax.ShapeDtypeStruct((ROWS, GROUP), BF16),
            jax.ShapeDtypeStruct((ROWS, GROUP), BF16),
            jax.ShapeDtypeStruct((ROWS, LANES), BF16),
        ],
        compiler_params=_cparams(("parallel",), 40),
        name="mla_prep",
    )(proj, proj, gain, w_ukv, cos, sin)


MLA_T = 512


def _mla_step(q, k, v, m, l, acc, mask):
    s = _dot_nt(q, k)
    if mask is not None:
        s = jnp.where(mask, s, -jnp.inf)
    m_new = jnp.maximum(m, jnp.max(s, axis=-1, keepdims=True))
    alpha = jnp.exp2(m - m_new)
    p = jnp.exp2(s - m_new)
    l = alpha * l + jnp.sum(p, axis=-1, keepdims=True)
    acc = alpha * acc + _dot(p.astype(BF16), v)
    return m_new, l, acc


def _mla_kernel(qn_ref, qpe_ref, kn_ref, kpe_ref, v_ref, o_ref, kc_ref):
    i = pl.program_id(2)
    heads = [slice(hh * HEAD_DIM, (hh + 1) * HEAD_DIM) for hh in range(ATT_HEADS)]

    @pl.when(i == 0)
    def _():
        for hh, hd in enumerate(heads):
            kc_ref[hh, :, :LANES] = kn_ref[:, hd]
            kc_ref[hh, :, LANES:] = kpe_ref[...]

    qs = [jnp.concatenate([(qn_ref[:, hd] * MLA_SCALE).astype(BF16), qpe_ref[:, hd]], axis=1)
          for hd in heads]
    row = lax.broadcasted_iota(jnp.int32, (MLA_T, MLA_T), 0)
    col = lax.broadcasted_iota(jnp.int32, (MLA_T, MLA_T), 1)

    def block(g, state, mask):
        k0 = pl.multiple_of(g * MLA_T, MLA_T)
        return tuple(
            _mla_step(q, kc_ref[hh, pl.ds(k0, MLA_T), :], v_ref[pl.ds(k0, MLA_T), hd], m, l, acc, mask)
            for hh, (q, hd, (m, l, acc)) in enumerate(zip(qs, heads, state)))

    init = (jnp.full((MLA_T, 1), -jnp.inf, F32), jnp.zeros((MLA_T, 1), F32), jnp.zeros((MLA_T, HEAD_DIM), F32))
    state = block(i, (init,) * ATT_HEADS, col <= row)
    state = lax.fori_loop(0, i, lambda t, s: block(t, s, None), state)
    for hd, (_, l, acc) in zip(heads, state):
        o_ref[:, hd] = (acc / l).astype(o_ref.dtype)


def _mla_attention(proj, kn, v, qpe, kpe):
    nq = SEQ // MLA_T
    hw = ATT_HEADS * HEAD_DIM
    hs = HEADS // ATT_HEADS
    return pl.pallas_call(
        _mla_kernel,
        grid=(BATCH, hs, nq),
        in_specs=[
            pl.BlockSpec((MLA_T, hw), lambda b, h, i: (b * nq + i, G_MQN * hs + h)),
            pl.BlockSpec((MLA_T, hw), lambda b, h, i: (b * nq + i, h)),
            pl.BlockSpec((SEQ, hw), lambda b, h, i: (b, h)),
            pl.BlockSpec((SEQ, LANES), lambda b, h, i: (b, 0)),
            pl.BlockSpec((SEQ, hw), lambda b, h, i: (b, h)),
        ],
        out_specs=pl.BlockSpec((MLA_T, hw), lambda b, h, i: (b * nq + i, h)),
        out_shape=jax.ShapeDtypeStruct((ROWS, BRANCH_WIDTH), BF16),
        scratch_shapes=[pltpu.VMEM((ATT_HEADS, SEQ, 2 * LANES), BF16)],
        compiler_params=_cparams(("parallel", "parallel", "arbitrary"), 32),
        name="mla_attention",
    )(proj, qpe, kn, kpe, v)


def _rope128(x, cos, sin_signed):
    return x * cos + pltpu.roll(x, HEAD_DIM // 2, 1) * sin_signed


def _ret_kernel(q_ref, k_ref, v_ref, g_ref, cos_ref, sin_ref, dec_ref, qd_ref, kd_ref, cd_ref, o_ref):
    C = RET_CHUNK
    dec = dec_ref[...]
    qd = qd_ref[...]
    kd = kd_ref[...]
    cd = cd_ref[...]
    state = jnp.zeros((HEAD_DIM, HEAD_DIM), F32)
    for n in range(SEQ // C):
        sl = pl.ds(n * C, C)
        cos = cos_ref[sl, :]
        sin = sin_ref[sl, :]
        q = _rope128(q_ref[sl, :], cos, sin)
        k = _rope128(k_ref[sl, :], cos, sin) * (HEAD_DIM ** -0.5)
        v = v_ref[sl, :].astype(BF16)
        scores = _dot_nt(q.astype(BF16), k.astype(BF16)) * dec
        inner = _dot(scores.astype(BF16), v)
        cross = _dot((q * qd).astype(BF16), state.astype(BF16))
        state = state * cd + _dot_tn((k * kd).astype(BF16), v)
        o = inner + cross
        o = o * lax.rsqrt(jnp.mean(o * o, axis=-1, keepdims=True) + EPS)
        g = g_ref[sl, :]
        o_ref[sl, :] = (o * (g * jax.nn.sigmoid(g))).astype(o_ref.dtype)


def _ret_consts():
    C = RET_CHUNK
    log_gamma = jnp.log1p(-(2.0 ** (-5.0 - jnp.arange(HEADS, dtype=F32))))
    idx = jnp.arange(C, dtype=F32)
    diff = idx[:, None] - idx[None, :]
    decay = jnp.where(diff >= 0, jnp.exp(log_gamma[:, None, None] * jnp.maximum(diff, 0.0)), 0.0)
    k_decay = jnp.exp(log_gamma[:, None] * (C - 1.0 - idx)[None, :])
    q_decay = jnp.exp(log_gamma[:, None] * (idx + 1.0)[None, :])
    chunk_decay = jnp.exp(log_gamma * C)
    bc = lambda t: jnp.broadcast_to(t[:, :, None], (HEADS, C, HEAD_DIM))
    cdm = jnp.broadcast_to(chunk_decay[:, None, None], (HEADS, HEAD_DIM, HEAD_DIM))
    return decay, bc(q_decay), bc(k_decay), cdm


def _retention(proj, cos, sin, consts):
    seq = lambda g: pl.BlockSpec((SEQ, HEAD_DIM), lambda b, h: (b, g * HB + h))
    tab = pl.BlockSpec((SEQ, LANES), lambda b, h: (b, 0))
    cst = pl.BlockSpec((None, HEAD_DIM, HEAD_DIM), lambda b, h: (h, 0, 0))
    return pl.pallas_call(
        _ret_kernel,
        grid=(BATCH, HEADS),
        in_specs=[seq(G_RQ), seq(G_RK), seq(G_RV), seq(G_RG), tab, tab, cst, cst, cst, cst],
        out_specs=pl.BlockSpec((SEQ, HEAD_DIM), lambda b, h: (b, h)),
        out_shape=jax.ShapeDtypeStruct((ROWS, BRANCH_WIDTH), BF16),
        compiler_params=_cparams(("parallel", "parallel"), 32),
        name="retention",
    )(proj, proj, proj, proj, cos, sin, *consts)


LRU_TS = 256
LRU_PAD = 8


def _lru_kernel(x_ref, y_ref, cw_ref, cb_ref, wa_ref, ba_ref, wx_ref, bx_ref, lam_ref, o_ref,
                xp_ref, p_ref, q_ref):
    W = LRU_WIDTH
    xp_ref[0:LRU_PAD, :] = jnp.zeros((LRU_PAD, W), F32)
    cw = cw_ref[...]
    cb = cb_ref[...]
    ba = ba_ref[...]
    bx = bx_ref[...]
    nlam = -lam_ref[...]
    sp = jnp.maximum(nlam, 0.0) + jnp.log1p(jnp.exp(-jnp.abs(nlam)))
    sub = lax.broadcasted_iota(jnp.int32, (LRU_TS // SUBLANES, SUBLANES, W), 1)

    def copy(c, _):
        r0 = pl.multiple_of(c * LRU_TS, LRU_TS)
        xp_ref[pl.ds(LRU_PAD + r0, LRU_TS), :] = x_ref[pl.ds(r0, LRU_TS), :]
        return 0

    lax.fori_loop(0, SEQ // LRU_TS, copy, 0)

    def gates(c, _):
        r0 = pl.multiple_of(c * LRU_TS, LRU_TS)
        win = xp_ref[pl.ds(r0, LRU_TS + LRU_PAD), :].reshape(LRU_TS // SUBLANES + 1, SUBLANES, W)
        xc = cb
        for tap in range(CONV_WIDTH):
            s = CONV_WIDTH - 1 - tap
            rolled = pltpu.roll(win, s, 1) if s else win
            shifted = jnp.where(sub >= s, rolled[1:], rolled[:-1]) if s else win[1:]
            xc = xc + shifted * cw[tap:tap + 1, :]
        xc = xc.reshape(LRU_TS, W)
        ra, ri = [], []
        for blk in range(LRU_BLOCKS):
            xb = xc[:, blk * LRU_BLOCK:(blk + 1) * LRU_BLOCK].astype(BF16)
            ra.append(_dot(xb, wa_ref[blk]))
            ri.append(_dot(xb, wx_ref[blk]))
        r = jax.nn.sigmoid(jnp.concatenate(ra, axis=1) + ba)
        gi = jax.nn.sigmoid(jnp.concatenate(ri, axis=1) + bx)
        log_a = -LRU_C * r * sp
        a = jnp.exp(log_a)
        t = jnp.tanh(log_a)
        b = jnp.sqrt(-2.0 * t / (1.0 - t)) * (gi * xc)
        a = a.reshape(LRU_TS // SUBLANES, SUBLANES, W)
        b = b.reshape(LRU_TS // SUBLANES, SUBLANES, W)
        for k in (1, 2, 4):
            keep = sub >= k
            b = jnp.where(keep, b + a * pltpu.roll(b, k, 1), b)
            a = jnp.where(keep, a * pltpu.roll(a, k, 1), a)
        p_ref[pl.ds(r0, LRU_TS), :] = a.reshape(LRU_TS, W)
        q_ref[pl.ds(r0, LRU_TS), :] = b.reshape(LRU_TS, W)
        return 0

    lax.fori_loop(0, SEQ // LRU_TS, gates, 0)

    def carry(gidx, h):
        r0 = pl.multiple_of(gidx * SUBLANES, SUBLANES)
        rows = p_ref[pl.ds(r0, SUBLANES), :] * h + q_ref[pl.ds(r0, SUBLANES), :]
        q_ref[pl.ds(r0, SUBLANES), :] = rows
        return jnp.broadcast_to(rows[SUBLANES - 1:SUBLANES, :], (SUBLANES, W))

    lax.fori_loop(0, SEQ // SUBLANES, carry, jnp.zeros((SUBLANES, W), F32), unroll=8)

    def out(c, _):
        r0 = pl.multiple_of(c * LRU_TS, LRU_TS)
        y = y_ref[pl.ds(r0, LRU_TS), :]
        o_ref[pl.ds(r0, LRU_TS), :] = (q_ref[pl.ds(r0, LRU_TS), :] * jax.nn.gelu(y)).astype(o_ref.dtype)
        return 0

    lax.fori_loop(0, SEQ // LRU_TS, out, 0)


def _rglru(proj, conv_w, conv_b, w_a, b_a, w_x, b_x, lam, l):
    W = LRU_WIDTH
    vec = _layer_vec(W, l)
    blk = pl.BlockSpec((None, LRU_BLOCKS, LRU_BLOCK, LRU_BLOCK), lambda b: (l, 0, 0, 0))
    return pl.pallas_call(
        _lru_kernel,
        grid=(BATCH,),
        in_specs=[
            pl.BlockSpec((SEQ, W), lambda b: (b, G_LX)),
            pl.BlockSpec((SEQ, W), lambda b: (b, G_LY)),
            pl.BlockSpec((None, CONV_WIDTH, W), lambda b: (l, 0, 0)),
            vec, blk, vec, blk, vec, vec,
        ],
        out_specs=pl.BlockSpec((SEQ, W), lambda b: (b, 0)),
        out_shape=jax.ShapeDtypeStruct((ROWS, W), BF16),
        scratch_shapes=[
            pltpu.VMEM((SEQ + LRU_PAD, W), F32),
            pltpu.VMEM((SEQ, W), F32),
            pltpu.VMEM((SEQ, W), F32),
        ],
        compiler_params=_cparams(("parallel",), 48),
        name="rglru",
    )(proj, proj, conv_w, conv_b, w_a, b_a, w_x, b_x, lam)


MERGE_TM, MERGE_TN = 1024, 512


def _merge_kernel(h_ref, o0_ref, o1_ref, o2_ref, o3_ref, wg_ref, wb_ref, out_ref):
    h = h_ref[...]
    acc = None
    for br, o_ref in enumerate((o0_ref, o1_ref, o2_ref, o3_ref)):
        t = jax.nn.sigmoid(_dot(h, wg_ref[br])) * _dot(o_ref[...], wb_ref[br])
        acc = t if acc is None else acc + t
    out_ref[...] = acc.astype(out_ref.dtype)


def _merge(h, branches, w_gate, w_branch, l):
    tm, tn = MERGE_TM, MERGE_TN
    obs = pl.BlockSpec((tm, BRANCH_WIDTH), lambda i, j: (i, 0))
    return pl.pallas_call(
        _merge_kernel,
        grid=(ROWS // tm, D_MODEL // tn),
        in_specs=[
            pl.BlockSpec((tm, D_MODEL), lambda i, j: (i, 0)),
            obs, obs, obs, obs,
            pl.BlockSpec((None, N_BRANCH, D_MODEL, tn), lambda i, j: (l, 0, 0, j)),
            pl.BlockSpec((None, N_BRANCH, BRANCH_WIDTH, tn), lambda i, j: (l, 0, 0, j)),
        ],
        out_specs=pl.BlockSpec((tm, tn), lambda i, j: (i, j)),
        out_shape=jax.ShapeDtypeStruct((ROWS, D_MODEL), BF16),
        compiler_params=_cparams(("parallel", "arbitrary"), 56),
        name="gated_merge",
    )(h, *branches, w_gate, w_branch)


OUT_TM, OUT_TN = 512, D_MODEL


def _outproj_kernel(m_ref, w_ref, x_ref, o_ref):
    o_ref[...] = x_ref[...] + _dot(m_ref[...], w_ref[...])


def _outproj(mixed, w_out, x, l):
    tm, tn = OUT_TM, OUT_TN
    return pl.pallas_call(
        _outproj_kernel,
        grid=(ROWS // tm, D_MODEL // tn),
        in_specs=[
            pl.BlockSpec((tm, D_MODEL), lambda i, j: (i, 0)),
            pl.BlockSpec((None, D_MODEL, tn), lambda i, j: (l, 0, j)),
            pl.BlockSpec((tm, tn), lambda i, j: (i, j)),
        ],
        out_specs=pl.BlockSpec((tm, tn), lambda i, j: (i, j)),
        out_shape=jax.ShapeDtypeStruct((ROWS, D_MODEL), F32),
        compiler_params=_cparams(("parallel", "arbitrary"), 44),
        name="out_proj",
    )(mixed, w_out, x)


FFN_TM, FFN_TF = 1024, 512
FFN_STEPS = pl.cdiv(D_FF, FFN_TF)
FFN_TAIL = D_FF - (FFN_STEPS - 1) * FFN_TF
assert FFN_STEPS >= 2


def _ffn_kernel(x_ref, g_ref, wg_ref, wu_ref, wd_ref, gn_ref, *rest, final):
    if final:
        o_ref, h_ref = rest
        n_ref = o_ref
    else:
        o_ref, n_ref = rest
        h_ref = n_ref
    f = pl.program_id(1)

    def hidden_tile(width, base_ref):
        h = h_ref[...]
        a = _dot(h, wg_ref[:, :width])
        act = (a * jax.nn.sigmoid(a) * _dot(h, wu_ref[:, :width])).astype(BF16)
        o_ref[...] = base_ref[...] + _dot(act, wd_ref[:width, :])

    @pl.when(f == 0)
    def _():
        _norm_into(x_ref, g_ref, h_ref, FFN_TM)
        hidden_tile(FFN_TF, x_ref)

    pl.when(jnp.logical_and(f > 0, f < FFN_STEPS - 1))(lambda: hidden_tile(FFN_TF, o_ref))

    @pl.when(f == FFN_STEPS - 1)
    def _():
        hidden_tile(FFN_TAIL, o_ref)
        _norm_into(o_ref, gn_ref, n_ref, FFN_TM)


def _ffn(x, ln, wg, wu, wd, l, ln_next, l_next, final):
    tm, tf = FFN_TM, FFN_TF
    row = pl.BlockSpec((tm, D_MODEL), lambda i, f: (i, 0))
    out_f32 = jax.ShapeDtypeStruct((ROWS, D_MODEL), F32)
    return pl.pallas_call(
        functools.partial(_ffn_kernel, final=final),
        grid=(ROWS // tm, FFN_STEPS),
        in_specs=[
            row,
            _layer_vec(D_MODEL, l),
            pl.BlockSpec((None, D_MODEL, tf), lambda i, f: (l, 0, f)),
            pl.BlockSpec((None, D_MODEL, tf), lambda i, f: (l, 0, f)),
            pl.BlockSpec((None, tf, D_MODEL), lambda i, f: (l, f, 0)),
            _layer_vec(D_MODEL, l_next),
        ],
        out_specs=row if final else [row, row],
        out_shape=out_f32 if final else [out_f32, jax.ShapeDtypeStruct((ROWS, D_MODEL), BF16)],
        scratch_shapes=[pltpu.VMEM((tm, D_MODEL), BF16)] if final else [],
        compiler_params=_cparams(("parallel", "arbitrary"), 62),
        name="ffn_final" if final else "ffn",
    )(x, ln, wg, wu, wd, ln_next)


def _pack_w_in(w):
    w = w.astype(BF16)
    lead = w.shape[:-1]
    mq0 = 3 * GROUP
    mq = w[..., mq0:mq0 + HEADS * (MLA_NOPE + MLA_ROPE)].reshape(*lead, HEADS, MLA_NOPE + MLA_ROPE)
    ckv0 = mq0 + HEADS * (MLA_NOPE + MLA_ROPE)
    kr0 = ckv0 + MLA_KV_RANK
    rest0 = kr0 + MLA_ROPE
    pe_pad = GROUP - HEADS * MLA_ROPE - MLA_ROPE
    parts = [
        w[..., :mq0],
        mq[..., :MLA_NOPE].reshape(*lead, HEADS * MLA_NOPE),
        w[..., ckv0:kr0],
        mq[..., MLA_NOPE:].reshape(*lead, HEADS * MLA_ROPE),
        w[..., kr0:rest0],
        jnp.zeros((*lead, pe_pad), w.dtype),
        w[..., rest0:],
    ]
    return jnp.concatenate(parts, axis=-1)


def _pack_w_ukv(w):
    w = w.astype(BF16)
    lead = w.shape[:-1]
    t = w.reshape(*lead, HEADS, 2 * HEAD_DIM)
    return jnp.concatenate([t[..., :HEAD_DIM].reshape(*lead, GROUP), t[..., HEAD_DIM:].reshape(*lead, GROUP)],
                           axis=-1)


def kernel(x, positions, ln1, w_in, mla_kv_gain, mla_w_ukv, lru_conv_w, lru_conv_b, lru_w_a, lru_b_a,
           lru_w_x, lru_b_x, lru_lambda, w_branch, w_gate, w_out, ln2, ffn_w_gate, ffn_w_up,
           ffn_w_down, ln_final):
    ret_cos, ret_sin, mla_cos, mla_sin = _rope_tables(positions)
    ret_consts = _ret_consts()
    vec = lambda t: t.reshape(DEPTH, 1, -1)
    w_in_p = _pack_w_in(w_in)
    w_ukv_p = _pack_w_ukv(mla_w_ukv)
    w_a, w_x = lru_w_a.astype(BF16), lru_w_x.astype(BF16)
    w_gate_b, w_branch_b, w_out_b = w_gate.astype(BF16), w_branch.astype(BF16), w_out.astype(BF16)
    wf_gate, wf_up, wf_down = ffn_w_gate.astype(BF16), ffn_w_up.astype(BF16), ffn_w_down.astype(BF16)
    ln1, ln2, kv_gain = vec(ln1), vec(ln2), vec(mla_kv_gain)
    conv_b, b_a, b_x, lam = vec(lru_conv_b), vec(lru_b_a), vec(lru_b_x), vec(lru_lambda)

    xf = x.reshape(ROWS, D_MODEL)
    h = _norm(xf, ln1, 0)
    for l in range(DEPTH):
        proj = _proj(h, w_in_p, l)
        o_sb = _sb_attention(proj)
        kn, v, qpe, kpe = _mla_prep(proj, kv_gain, w_ukv_p, mla_cos, mla_sin, l)
        o_mla = _mla_attention(proj, kn, v, qpe, kpe)
        o_ret = _retention(proj, ret_cos, ret_sin, ret_consts)
        o_lru = _rglru(proj, lru_conv_w, conv_b, w_a, b_a, w_x, b_x, lam, l)
        mixed = _merge(h, (o_sb, o_mla, o_ret, o_lru), w_gate_b, w_branch_b, l)
        xf = _outproj(mixed, w_out_b, xf, l)
        if l < DEPTH - 1:
            xf, h = _ffn(xf, ln2, wf_gate, wf_up, wf_down, l, ln1, l + 1, final=False)
        else:
            xf = _ffn(xf, ln2, wf_gate, wf_up, wf_down, l, ln_final.reshape(1, 1, D_MODEL), 0, final=True)
    return xf.reshape(BATCH, SEQ, D_MODEL)
```
